```python
import math, functools
import jax, jax.numpy as jnp
from jax import lax
import numpy as np


D_MODEL = 1024
BATCH = 16
SEQ = 2048
DEPTH = 4

GRID_W = 64
CTX_LEN = 256
MIX_W = D_MODEL
N_MIXERS = 4
BR_W = MIX_W // N_MIXERS
HEAD_DIM = 64
RET_HEADS = BR_W // HEAD_DIM
SG_HEADS = BR_W // HEAD_DIM
GDN_HEADS = BR_W // HEAD_DIM
RET_CHUNK = 128
SG_CHUNK = 128
GDN_CHUNK = 64
CONV_W = 3
ROPE_BASE = 10000.0
EPS = 1e-6
SEGMENTS = (BR_W,) * 15 + (GDN_HEADS,) * 4
IN_W = 15 * BR_W + 4 * GDN_HEADS

kernel_name = 'hybrid_parallel_mixer_dit_block'


def rms_norm(x, g):
    xf = x.astype(jnp.float32)
    y = xf * lax.rsqrt(jnp.mean(xf * xf, axis=-1, keepdims=True) + EPS)
    return (y * g.astype(jnp.float32)).astype(x.dtype)


def layer_norm_plain(x):
    xf = x.astype(jnp.float32)
    mu = jnp.mean(xf, axis=-1, keepdims=True)
    var = jnp.mean(jnp.square(xf - mu), axis=-1, keepdims=True)
    return ((xf - mu) * lax.rsqrt(var + EPS)).astype(x.dtype)


def l2_normalize(x):
    return x * lax.rsqrt(jnp.sum(x * x, axis=-1, keepdims=True) + EPS)


def dwconv_centred(x, w):
    k_w = w.shape[0]
    t = x.shape[1]
    xp = jnp.pad(x, ((0, 0), (k_w // 2, k_w // 2), (0, 0)))
    out = xp[:, 0:t] * w[0]
    for i in range(1, k_w):
        out = out + xp[:, i:i + t] * w[i]
    return out


def grid_rotary(x, row, col):
    half = x.shape[-1] // 2
    nf = half // 2
    inv = ROPE_BASE ** (-jnp.arange(nf, dtype=jnp.float32) / nf)

    def rot(xs, pos):
        ang = pos.astype(jnp.float32)[:, None] * inv
        cos, sin = jnp.cos(ang)[None, :, None, :], jnp.sin(ang)[None, :, None, :]
        x1, x2 = xs[..., :nf], xs[..., nf:]
        return jnp.concatenate([x1 * cos - x2 * sin, x1 * sin + x2 * cos], axis=-1)

    return jnp.concatenate([rot(x[..., :half], row), rot(x[..., half:], col)], axis=-1)


def split_proj(p):
    cuts = [int(i) for i in np.cumsum(SEGMENTS)[:-1]]
    return jnp.split(p, cuts, axis=-1)


def identity(a):
    return a


flip_time = functools.partial(jnp.flip, axis=1)


def retention_scan(q, k, v, log_gamma, s0, with_output):
    bsz, t, h, dk = k.shape
    n = t // RET_CHUNK

    def chunks(a):
        return a.reshape(bsz, n, RET_CHUNK, h, a.shape[-1]).transpose(1, 0, 3, 2, 4)

    pos = jnp.arange(RET_CHUNK, dtype=jnp.float32)
    lg = log_gamma[:, None]
    k_dec = jnp.exp((RET_CHUNK - 1.0 - pos) * lg)[..., None]
    c_dec = jnp.exp(RET_CHUNK * log_gamma)[:, None, None]
    kc, vc = chunks(k * dk ** -0.5), chunks(v)

    def update(s, ki, vi):
        return s * c_dec + jnp.einsum('bhcd,bhce->bhde', ki * k_dec, vi)

    if not with_output:
        s_fin, _ = lax.scan(lambda s, kv: (update(s, kv[0], kv[1]), None), s0, (kc, vc))
        return None, s_fin
    diff = pos[:, None] - pos[None, :]
    intra = jnp.exp(jnp.where(diff >= 0, diff * lg[..., None], -jnp.inf))
    q_dec = jnp.exp((pos + 1.0) * lg)[..., None]

    def step(s, inp):
        qi, ki, vi = inp
        scores = jnp.einsum('bhid,bhjd->bhij', qi, ki) * intra
        o = jnp.einsum('bhij,bhje->bhie', scores, vi) + jnp.einsum('bhid,bhde->bhie', qi, s) * q_dec
        return update(s, ki, vi), o

    s_fin, o = lax.scan(step, s0, (chunks(q), kc, vc))
    return o.transpose(1, 0, 3, 2, 4).reshape(bsz, t, h, -1), s_fin


def gated_delta_scan(q, k, v, g, beta, s0, with_output):
    bsz, t, h, dk = k.shape
    dv = v.shape[-1]
    n = t // GDN_CHUNK

    def chunks(a):
        return a.reshape(bsz, n, GDN_CHUNK, h, -1).transpose(1, 0, 3, 2, 4)

    kc, vc = chunks(k), chunks(v)
    bc = chunks(beta[..., None])
    gc = jnp.cumsum(chunks(g[..., None])[..., 0], axis=-1)
    idx = jnp.arange(GDN_CHUNK)
    diff = gc[..., :, None] - gc[..., None, :]
    decay = jnp.exp(jnp.where(idx[:, None] >= idx[None, :], diff, -jnp.inf))
    kb = kc * bc
    lower = jnp.where(idx[:, None] > idx[None, :],
                      jnp.einsum('nbhid,nbhjd->nbhij', kb, kc) * decay, 0.0)
    rhs = jnp.concatenate([vc * bc, kb * jnp.exp(gc)[..., None]], axis=-1)
    sol = lax.linalg.triangular_solve(lower, rhs, left_side=True, lower=True, unit_diagonal=True)
    uc, wc = sol[..., :dv], sol[..., dv:]
    g_last = gc[..., -1:]
    k_tail = kc * jnp.exp(g_last - gc)[..., None]
    c_dec = jnp.exp(g_last)[..., None]

    def update(s, ui, wi, kti, cdi):
        v_new = ui - jnp.einsum('bhck,bhkv->bhcv', wi, s)
        return s * cdi + jnp.einsum('bhck,bhcv->bhkv', kti, v_new), v_new

    if not with_output:
        s_fin, _ = lax.scan(lambda s, xs: (update(s, *xs)[0], None), s0, (uc, wc, k_tail, c_dec))
        return None, s_fin
    qc = chunks(q * dk ** -0.5)
    qd = qc * jnp.exp(gc)[..., None]

    def step(s, inp):
        qi, qdi, ki, di, ui, wi, kti, cdi = inp
        s_new, v_new = update(s, ui, wi, kti, cdi)
        intra = jnp.einsum('bhik,bhjk->bhij', qi, ki) * di
        o = jnp.einsum('bhck,bhkv->bhcv', qdi, s) + jnp.einsum('bhij,bhjv->bhiv', intra, v_new)
        return s_new, o

    s_fin, o = lax.scan(step, s0, (qc, qd, kc, decay, uc, wc, k_tail, c_dec))
    return o.transpose(1, 0, 3, 2, 4).reshape(bsz, t, h, dv), s_fin


def retention_mixer(lat, ctx, row, col, norm_g, ctx_out):
    def heads(a):
        return a.astype(jnp.float32).reshape(a.shape[0], a.shape[1], RET_HEADS, HEAD_DIM)

    ql, kl, vl = grid_rotary(heads(lat[0]), row, col), grid_rotary(heads(lat[1]), row, col), heads(lat[2])
    qc, kc, vc = heads(ctx[0]), heads(ctx[1]), heads(ctx[2])
    log_gamma = jnp.log(1.0 - 2.0 ** (-5.0 - jnp.arange(RET_HEADS, dtype=jnp.float32)))
    s0 = jnp.zeros((ql.shape[0], RET_HEADS, HEAD_DIM, HEAD_DIM), jnp.float32)
    o_lat, o_ctx = 0.0, 0.0
    for d in (identity, flip_time):
        oc, sc = retention_scan(d(qc), d(kc), d(vc), log_gamma, s0, ctx_out)
        ol, _ = retention_scan(d(ql), d(kl), d(vl), log_gamma, sc, True)
        o_lat = o_lat + d(ol)
        if ctx_out:
            o_ctx = o_ctx + d(oc)

    def finish(o, z):
        mu = jnp.mean(o, axis=-1, keepdims=True)
        var = jnp.mean(jnp.square(o - mu), axis=-1, keepdims=True)
        y = ((o - mu) * lax.rsqrt(var + EPS)).reshape(o.shape[0], o.shape[1], BR_W) * norm_g
        return (y * jax.nn.silu(z.astype(jnp.float32))).astype(z.dtype)

    return finish(o_lat, lat[3]), (finish(o_ctx, ctx[3]) if ctx_out else None)


def spatial_gating(u, v, z, w_s, b_s):
    bsz, t, _ = u.shape
    n = t // SG_CHUNK
    u = jax.nn.gelu(u)
    v = layer_norm_plain(jax.nn.gelu(v)).reshape(bsz, n, SG_CHUNK, SG_HEADS, HEAD_DIM)
    s = jnp.einsum('hij,bnjhd->bnihd', w_s, v) + b_s.T[:, :, None]
    return u * s.reshape(bsz, t, BR_W) * jax.nn.silu(z)


def short_conv(b, c, h, z, w):
    return b * dwconv_centred(c * h, w) * jax.nn.silu(z)


def gdn_mixer(lat, ctx, conv_w, a_log, dt_bias, norm_g, ctx_out):
    def prep(parts):
        q, k, v, _, a_f, a_b, b_f, b_b = parts
        qkv = jax.nn.silu(dwconv_centred(jnp.concatenate([q, k, v], axis=-1), conv_w)).astype(jnp.float32)
        bsz, t, _ = qkv.shape
        q, k, v = [a.reshape(bsz, t, GDN_HEADS, HEAD_DIM) for a in jnp.split(qkv, 3, axis=-1)]
        g = [-jnp.exp(a_log[i]) * jax.nn.softplus(a.astype(jnp.float32) + dt_bias[i])
             for i, a in enumerate((a_f, a_b))]
        beta = [jax.nn.sigmoid(b.astype(jnp.float32)) for b in (b_f, b_b)]
        return l2_normalize(q), l2_normalize(k), v, g, beta

    ql, kl, vl, gl, bl = prep(lat)
    qc, kc, vc, gcx, bcx = prep(ctx)
    s0 = jnp.zeros((ql.shape[0], GDN_HEADS, HEAD_DIM, HEAD_DIM), jnp.float32)
    o_lat, o_ctx = 0.0, 0.0
    for i, d in enumerate((identity, flip_time)):
        oc, sc = gated_delta_scan(d(qc), d(kc), d(vc), d(gcx[i]), d(bcx[i]), s0, ctx_out)
        ol, _ = gated_delta_scan(d(ql), d(kl), d(vl), d(gl[i]), d(bl[i]), sc, True)
        o_lat = o_lat + d(ol)
        if ctx_out:
            o_ctx = o_ctx + d(oc)

    def finish(o, z):
        y = o * lax.rsqrt(jnp.mean(o * o, axis=-1, keepdims=True) + EPS) * norm_g
        y = y.reshape(o.shape[0], o.shape[1], BR_W)
        return (y * jax.nn.silu(z.astype(jnp.float32))).astype(z.dtype)

    return finish(o_lat, lat[3]), (finish(o_ctx, ctx[3]) if ctx_out else None)


def token_mixers(p_lat, p_ctx, row, col, ret_norm_g, sg_w, sg_b, sc_conv_w,
                 gdn_conv_w, gdn_a_log, gdn_dt_bias, gdn_norm_g, ctx_out):
    lat, cx = split_proj(p_lat), split_proj(p_ctx)
    a_lat, a_ctx = retention_mixer(lat[0:4], cx[0:4], row, col, ret_norm_g, ctx_out)
    d_lat, d_ctx = gdn_mixer(lat[11:19], cx[11:19], gdn_conv_w, gdn_a_log, gdn_dt_bias, gdn_norm_g, ctx_out)
    y_lat = jnp.concatenate([a_lat, spatial_gating(*lat[4:7], sg_w, sg_b),
                             short_conv(*lat[7:11], sc_conv_w), d_lat], axis=-1).astype(p_lat.dtype)
    if not ctx_out:
        return y_lat, None
    y_ctx = jnp.concatenate([a_ctx, spatial_gating(*cx[4:7], sg_w, sg_b),
                             short_conv(*cx[7:11], sc_conv_w), d_ctx], axis=-1).astype(p_ctx.dtype)
    return y_lat, y_ctx


def setup_inputs(seed: int = 0) -> dict:
    key = jax.random.key(seed)
    ks = jax.random.split(key, 18)

    def nrm(k, shape, s):
        return jax.random.normal(k, shape, jnp.float32) * s

    dt = jnp.exp(jax.random.uniform(ks[16], (DEPTH, 2, GDN_HEADS), jnp.float32,
                                    minval=math.log(1e-3), maxval=math.log(1e-1)))
    return {
        'x': nrm(ks[0], (BATCH, SEQ, D_MODEL), 1.0),
        'c': nrm(ks[1], (BATCH, D_MODEL), 1.0),
        'ctx': nrm(ks[2], (BATCH, CTX_LEN, D_MODEL), 1.0),
        'c_ctx': nrm(ks[3], (D_MODEL,), 1.0),
        'w_mod': nrm(ks[4], (DEPTH, D_MODEL, 3 * D_MODEL), 0.5 * D_MODEL ** -0.5),
        'b_mod': nrm(ks[5], (DEPTH, 3 * D_MODEL), 0.02),
        'g_pre': 1.0 + nrm(ks[6], (DEPTH, D_MODEL), 0.02),
        'g_post': 1.0 + nrm(ks[7], (DEPTH, D_MODEL), 0.02),
        'w_in': nrm(ks[8], (DEPTH, D_MODEL, IN_W), D_MODEL ** -0.5),
        'w_out': nrm(ks[9], (DEPTH, MIX_W, D_MODEL), MIX_W ** -0.5),
        'ret_norm_g': 1.0 + nrm(ks[10], (DEPTH, BR_W), 0.02),
        'sg_w': nrm(ks[11], (DEPTH, SG_HEADS, SG_CHUNK, SG_CHUNK), SG_CHUNK ** -0.5),
        'sg_b': 1.0 + nrm(ks[12], (DEPTH, SG_HEADS, SG_CHUNK), 0.1),
        'sc_conv_w': nrm(ks[13], (DEPTH, CONV_W, BR_W), CONV_W ** -0.5),
        'gdn_conv_w': nrm(ks[14], (DEPTH, CONV_W, 3 * BR_W), CONV_W ** -0.5),
        'gdn_a_log': jnp.log(jax.random.uniform(ks[15], (DEPTH, 2, GDN_HEADS), jnp.float32, minval=1.0, maxval=16.0)),
        'gdn_dt_bias': dt + jnp.log(-jnp.expm1(-dt)),
        'gdn_norm_g': 1.0 + nrm(ks[17], (DEPTH, HEAD_DIM), 0.02),
    }


def reference(x, c, ctx, c_ctx, w_mod, b_mod, g_pre, g_post, w_in, w_out, ret_norm_g, sg_w, sg_b,
              sc_conv_w, gdn_conv_w, gdn_a_log, gdn_dt_bias, gdn_norm_g):
    rows = x.shape[1] // GRID_W
    row = jnp.repeat(jnp.arange(rows), GRID_W)
    col = jnp.tile(jnp.arange(GRID_W), rows)
    silu_c = jax.nn.silu(c)
    silu_cc = jax.nn.silu(c_ctx)
    for l in range(DEPTH):
        ctx_out = l < DEPTH - 1
        shift, scale, gate = jnp.split(silu_c @ w_mod[l] + b_mod[l], 3, axis=-1)
        shift_c, scale_c, gate_c = jnp.split(silu_cc @ w_mod[l] + b_mod[l], 3, axis=-1)
        h = rms_norm(x, g_pre[l]) * (1.0 + scale[:, None]) + shift[:, None]
        hc = rms_norm(ctx, g_pre[l]) * (1.0 + scale_c) + shift_c
        y, yc = token_mixers(h @ w_in[l], hc @ w_in[l], row, col, ret_norm_g[l], sg_w[l], sg_b[l],
                             sc_conv_w[l], gdn_conv_w[l], gdn_a_log[l], gdn_dt_bias[l], gdn_norm_g[l], ctx_out)
        x = x + gate[:, None] * rms_norm(y @ w_out[l], g_post[l])
        if ctx_out:
            ctx = ctx + gate_c * rms_norm(yc @ w_out[l], g_post[l])
    return x
```

```python
import functools

import jax
import jax.numpy as jnp
from jax import lax
from jax.experimental import pallas as pl
from jax.experimental.pallas import tpu as pltpu

HEAD_DIM = 64
N_HEADS = 4
BR_W = N_HEADS * HEAD_DIM
GRID_W = 64
RET_CHUNK = 128
GDN_CHUNK = 64
ROPE_BASE = 10000.0
EPS = 1e-6
N_GATES = 16
LANES = 128
SUBLANES = 8
SCAN_W = 6 * BR_W + LANES
FIN_W = 9 * BR_W
VMEM_LIMIT = 56 * 1024 * 1024

F32 = jnp.float32
BF16 = jnp.bfloat16
NEG_INF = float("-inf")


def _mm(a, b):
    return jnp.dot(a.astype(BF16), b.astype(BF16), preferred_element_type=F32)


def _mm_nt(a, b):
    return lax.dot_general(a.astype(BF16), b.astype(BF16), (((1,), (1,)), ((), ())),
                           preferred_element_type=F32)


def _mm_tn(a, b):
    return lax.dot_general(a.astype(BF16), b.astype(BF16), (((0,), (0,)), ((), ())),
                           preferred_element_type=F32)


def _iota(shape, dim):
    return lax.broadcasted_iota(jnp.int32, shape, dim)


def _silu(x):
    return x * jax.nn.sigmoid(x)


def _gelu_tanh(x):
    return 0.5 * x * (1.0 + jnp.tanh(0.7978845608028654 * (x + 0.044715 * (x * x * x))))


def _head_mean_matrix():
    r = _iota((BR_W, BR_W), 0) >> 6
    c = _iota((BR_W, BR_W), 1) >> 6
    return jnp.where(r == c, 1.0 / HEAD_DIM, 0.0).astype(F32)


def _blockdiag(x, chunk):
    shift = chunk.bit_length() - 1
    t = jnp.concatenate([x] * N_HEADS, axis=0)
    keep = (_iota(t.shape, 0) >> shift) == (_iota(t.shape, 1) >> 6)
    return jnp.where(keep, t, 0.0)


def _mod_kernel(cc_ref, w_ref, b_ref, o_ref):
    s = _silu(cc_ref[...])
    o_ref[...] = _mm(s, w_ref[...]) + b_ref[...]


def _modulation(cc, w_mod, b_mod):
    depth, d, _ = w_mod.shape
    bp = cc.shape[0]
    return pl.pallas_call(
        _mod_kernel,
        grid=(depth, 3),
        in_specs=[
            pl.BlockSpec((bp, d), lambda l, n: (0, 0)),
            pl.BlockSpec((None, d, d), lambda l, n: (l, 0, n)),
            pl.BlockSpec((None, None, 1, d), lambda l, n: (l, n, 0, 0)),
        ],
        out_specs=pl.BlockSpec((None, None, bp, d), lambda l, n: (l, n, 0, 0)),
        out_shape=jax.ShapeDtypeStruct((depth, 3, bp, d), F32),
        compiler_params=pltpu.CompilerParams(vmem_limit_bytes=VMEM_LIMIT),
        name="modulation",
    )(cc, w_mod, b_mod.reshape(depth, 3, 1, d))


def _inproj_kernel(x_ref, shift_ref, scale_ref, g_ref, w_ref, ps_ref, pf_ref):
    x = x_ref[...]
    y = x * lax.rsqrt(jnp.mean(x * x, axis=-1, keepdims=True) + EPS)
    h = (y * g_ref[...]) * (1.0 + scale_ref[...]) + shift_ref[...]
    hb = h.astype(BF16)
    ps_ref[...] = jnp.dot(hb, w_ref[:, :SCAN_W], preferred_element_type=F32)
    pf_ref[...] = jnp.dot(hb, w_ref[:, SCAN_W:], preferred_element_type=F32)


def _in_projection(xc, shift, scale, g_pre, w_in, *, tm, n_ctx_blocks):
    b, tt, d = xc.shape
    sel = lambda bi, i: (bi, (i >= n_ctx_blocks).astype(jnp.int32), 0, 0)
    return pl.pallas_call(
        _inproj_kernel,
        grid=(b, tt // tm),
        in_specs=[
            pl.BlockSpec((None, tm, d), lambda bi, i: (bi, i, 0)),
            pl.BlockSpec((None, None, 1, d), sel),
            pl.BlockSpec((None, None, 1, d), sel),
            pl.BlockSpec((1, d), lambda bi, i: (0, 0)),
            pl.BlockSpec((d, SCAN_W + FIN_W), lambda bi, i: (0, 0)),
        ],
        out_specs=[
            pl.BlockSpec((None, tm, SCAN_W), lambda bi, i: (bi, i, 0)),
            pl.BlockSpec((None, tm, FIN_W), lambda bi, i: (bi, i, 0)),
        ],
        out_shape=[
            jax.ShapeDtypeStruct((b, tt, SCAN_W), F32),
            jax.ShapeDtypeStruct((b, tt, FIN_W), F32),
        ],
        compiler_params=pltpu.CompilerParams(vmem_limit_bytes=VMEM_LIMIT),
        name="in_projection",
    )(xc, shift, scale, g_pre, w_in)


def _conv3(x, prev_row, next_row, w):
    t = x.shape[0]
    row = _iota(x.shape, 0)
    x_prev = jnp.where(row == 0, prev_row, pltpu.roll(x, 1, axis=0))
    x_next = jnp.where(row == t - 1, next_row, pltpu.roll(x, t - 1, axis=0))
    return x_prev * w[0:1, :] + x * w[1:2, :] + x_next * w[2:3, :]


def _halo_rows(prev_ref, next_ref, i, n_blocks, n_ctx_blocks):
    has_prev = jnp.logical_and(i > 0, i != n_ctx_blocks)
    has_next = jnp.logical_and(i < n_blocks - 1, i != n_ctx_blocks - 1)
    prev_row = jnp.where(has_prev, prev_ref[SUBLANES - 1:SUBLANES, :], 0.0)
    next_row = jnp.where(has_next, next_ref[0:1, :], 0.0)
    return prev_row, next_row


def _prep_kernel(ps_ref, prev_ref, next_ref, cos_ref, sin_ref, cw_ref, prm_ref,
                 qkv_ref, gate_ref, *, n_blocks, n_ctx_blocks):
    i = pl.program_id(1)
    ps = ps_ref[...]
    cos = cos_ref[...]
    sin = sin_ref[...]
    lane = _iota(cos.shape, 1)
    first_half = (lane & 31) < 16

    def rope(x):
        partner = jnp.where(first_half, -pltpu.roll(x, BR_W - 16, axis=1), pltpu.roll(x, 16, axis=1))
        return x * cos + partner * sin

    qkv_ref[:, 0 * BR_W:1 * BR_W] = rope(ps[:, 0 * BR_W:1 * BR_W]).astype(BF16)
    qkv_ref[:, 1 * BR_W:2 * BR_W] = (rope(ps[:, 1 * BR_W:2 * BR_W]) * HEAD_DIM ** -0.5).astype(BF16)
    qkv_ref[:, 2 * BR_W:3 * BR_W] = ps[:, 2 * BR_W:3 * BR_W].astype(BF16)

    prev_row, next_row = _halo_rows(prev_ref, next_ref, i, n_blocks, n_ctx_blocks)
    g = _silu(_conv3(ps[:, 3 * BR_W:6 * BR_W], prev_row, next_row, cw_ref[...]))
    hm = _head_mean_matrix() * HEAD_DIM

    def l2n(x):
        return x * lax.rsqrt(_mm(x * x, hm) + EPS)

    qkv_ref[:, 3 * BR_W:4 * BR_W] = (l2n(g[:, 0:BR_W]) * HEAD_DIM ** -0.5).astype(BF16)
    qkv_ref[:, 4 * BR_W:5 * BR_W] = l2n(g[:, BR_W:2 * BR_W]).astype(BF16)
    qkv_ref[:, 5 * BR_W:6 * BR_W] = g[:, 2 * BR_W:3 * BR_W].astype(BF16)

    a = ps[:, 6 * BR_W:]
    neg_rate = -jnp.exp(prm_ref[0:1, :])
    z = a + prm_ref[1:2, :]
    softplus = jnp.maximum(z, 0.0) + jnp.log1p(jnp.exp(-jnp.abs(z)))
    glane = _iota(a.shape, 1)
    gate_ref[...] = jnp.where(glane < N_GATES // 2, neg_rate * softplus, jax.nn.sigmoid(a))


def _prepare(ps, cos_tab, sin_tab, conv_w, prm, *, tb, n_ctx_blocks):
    b, tt, _ = ps.shape
    nb = tt // tb
    hb = tb // SUBLANES
    kern = functools.partial(_prep_kernel, n_blocks=nb, n_ctx_blocks=n_ctx_blocks)
    return pl.pallas_call(
        kern,
        grid=(b, nb),
        in_specs=[
            pl.BlockSpec((None, tb, SCAN_W), lambda bi, i: (bi, i, 0)),
            pl.BlockSpec((None, SUBLANES, 3 * BR_W), lambda bi, i: (bi, jnp.maximum(i * hb - 1, 0), 1)),
            pl.BlockSpec((None, SUBLANES, 3 * BR_W),
                         lambda bi, i: (bi, jnp.minimum((i + 1) * hb, tt // SUBLANES - 1), 1)),
            pl.BlockSpec((tb, BR_W), lambda bi, i: (i, 0)),
            pl.BlockSpec((tb, BR_W), lambda bi, i: (i, 0)),
            pl.BlockSpec((3, 3 * BR_W), lambda bi, i: (0, 0)),
            pl.BlockSpec((SUBLANES, LANES), lambda bi, i: (0, 0)),
        ],
        out_specs=[
            pl.BlockSpec((None, tb, 6 * BR_W), lambda bi, i: (bi, i, 0)),
            pl.BlockSpec((None, tb, LANES), lambda bi, i: (bi, i, 0)),
        ],
        out_shape=[
            jax.ShapeDtypeStruct((b, tt, 6 * BR_W), BF16),
            jax.ShapeDtypeStruct((b, tt, LANES), F32),
        ],
        compiler_params=pltpu.CompilerParams(vmem_limit_bytes=VMEM_LIMIT),
        name="prepare_scan_inputs",
    )(ps, ps, ps, cos_tab, sin_tab, conv_w, prm)


def _chunk_cumsum(x, chunk, reverse):
    t = x.shape[0]
    pos = _iota(x.shape, 0) & (chunk - 1)
    s = 1
    while s < chunk:
        if reverse:
            x = x + jnp.where(pos < chunk - s, pltpu.roll(x, t - s, axis=0), 0.0)
        else:
            x = x + jnp.where(pos >= s, pltpu.roll(x, s, axis=0), 0.0)
        s *= 2
    return x


def _expand_exact(x, e):
    hi = x.astype(BF16)
    r1 = x - hi.astype(F32)
    mid = r1.astype(BF16)
    lo = (r1 - mid.astype(F32)).astype(BF16)
    eb = e.astype(BF16)
    return (jnp.dot(hi, eb, preferred_element_type=F32) + jnp.dot(mid, eb, preferred_element_type=F32)
            + jnp.dot(lo, eb, preferred_element_type=F32))


def _retention_chunk(q, k, v, s, intra, qdec, kdec, cdec, bd_mask):
    scores = _mm_nt(q, _blockdiag(k, RET_CHUNK)) * intra
    o = _mm(scores, _blockdiag(v, RET_CHUNK)) + _mm(q, s) * qdec
    s_new = s * cdec + jnp.where(bd_mask, _mm_tn(k * kdec, v), 0.0)
    return o, s_new


def _unit_triangular_inverse(a, row, col, eye):
    c = GDN_CHUNK
    same16 = (row >> 4) == (col >> 4)
    same32 = (row >> 5) == (col >> 5)
    a16 = jnp.where(same16, a, 0.0)
    p = jnp.where(eye, 1.0, 0.0) - a16
    x = _mm(a16, _blockdiag(a16, c))
    for _ in range(2):
        px = _mm(jnp.concatenate([p, x], axis=0), _blockdiag(x, c))
        p = p + px[:c]
        x = px[c:]
    p = p + _mm(p, _blockdiag(x, c))
    for off in (jnp.where(jnp.logical_and(same32, jnp.logical_not(same16)), a, 0.0),
                jnp.where(same32, 0.0, a)):
        y = _mm(p, _blockdiag(off, c))
        p = p - _mm(y, _blockdiag(p, c))
    return p


def _gdn_chunk(q, k, v, gc, beta, s, reverse, bd_mask):
    c = GDN_CHUNK
    row = _iota((c, BR_W), 0)
    col = _iota((c, BR_W), 1) & (c - 1)
    eye = row == col
    incl = (row <= col) if reverse else (row >= col)
    strict = (row < col) if reverse else (row > col)
    gc_row = jnp.sum(jnp.where(eye, gc, 0.0), axis=0, keepdims=True)
    decay = jnp.exp(jnp.where(incl, gc - gc_row, NEG_INF))
    k_bd = _blockdiag(k, c)
    kk_qk = _mm_nt(jnp.concatenate([k, q], axis=0), k_bd)
    a = jnp.where(strict, kk_qk[:c] * beta * decay, 0.0)
    intra = kk_qk[c:] * decay

    p = _unit_triangular_inverse(a, row, col, eye)

    egc = jnp.exp(gc)
    kb = k * beta
    u = _mm(p, _blockdiag(v * beta, c))
    w = _mm(p, _blockdiag(kb * egc, c))
    ws_qs = _mm(jnp.concatenate([w, q * egc], axis=0), s)
    v_new = u - ws_qs[:c]
    o = ws_qs[c:] + _mm(intra, _blockdiag(v_new, c))
    g_last = gc[0:1, :] if reverse else gc[c - 1:c, :]
    k_tail = k * jnp.exp(g_last - gc)
    s_new = s * jnp.exp(g_last) + jnp.where(bd_mask, _mm_tn(k_tail, v_new), 0.0)
    return o, s_new


def _scan_kernel(qkv_f_ref, qkv_b_ref, gate_f_ref, gate_b_ref, intra_ref, qdec_ref, kdec_ref,
                 cdec_ref, o_f_ref, o_b_ref, s_ret_ref, s_gdn_ref):
    j = pl.program_id(1)

    @pl.when(j == 0)
    def _():
        s_ret_ref[...] = jnp.zeros_like(s_ret_ref)
        s_gdn_ref[...] = jnp.zeros_like(s_gdn_ref)

    bd_mask = (_iota((BR_W, BR_W), 0) >> 6) == (_iota((BR_W, BR_W), 1) >> 6)
    erow = _iota((LANES, BR_W), 0)
    ehead = _iota((LANES, BR_W), 1) >> 6
    c = GDN_CHUNK

    for d, (qkv_ref, gate_ref, o_ref) in enumerate(((qkv_f_ref, gate_f_ref, o_f_ref),
                                                    (qkv_b_ref, gate_b_ref, o_b_ref))):
        reverse = d == 1
        qkv = qkv_ref[...].astype(F32)
        o_ret, s_ret = _retention_chunk(qkv[:, 0:BR_W], qkv[:, BR_W:2 * BR_W], qkv[:, 2 * BR_W:3 * BR_W],
                                        s_ret_ref[d], intra_ref[d], qdec_ref[d], kdec_ref[d],
                                        cdec_ref[...], bd_mask)
        s_ret_ref[d] = s_ret
        o_ref[:, 0:BR_W] = o_ret

        gates = gate_ref[...]
        gc_all = _expand_exact(_chunk_cumsum(gates, c, reverse),
                               jnp.where(erow == N_HEADS * d + ehead, 1.0, 0.0))
        beta_all = _mm(gates, jnp.where(erow == N_GATES // 2 + N_HEADS * d + ehead, 1.0, 0.0))
        s_gdn = s_gdn_ref[d]
        n_sub = RET_CHUNK // c
        for sub in (range(n_sub - 1, -1, -1) if reverse else range(n_sub)):
            r0 = sub * c
            o_gdn, s_gdn = _gdn_chunk(qkv[r0:r0 + c, 3 * BR_W:4 * BR_W], qkv[r0:r0 + c, 4 * BR_W:5 * BR_W],
                                      qkv[r0:r0 + c, 5 * BR_W:6 * BR_W], gc_all[r0:r0 + c],
                                      beta_all[r0:r0 + c], s_gdn, reverse, bd_mask)
            o_ref[r0:r0 + c, BR_W:2 * BR_W] = o_gdn
        s_gdn_ref[d] = s_gdn


def _bwd_block(j, n_steps, n_ctx_steps):
    return jnp.where(j < n_ctx_steps, n_ctx_steps - 1 - j, n_steps - 1 - (j - n_ctx_steps))


def _scans(qkv, gates, intra, qdec, kdec, cdec, *, n_ctx_steps):
    b, tt, _ = qkv.shape
    c = RET_CHUNK
    ns = tt // c
    fwd = lambda bi, j: (bi, j, 0)
    bwd = lambda bi, j: (bi, _bwd_block(j, ns, n_ctx_steps), 0)
    const3 = lambda bi, j: (0, 0, 0)
    return pl.pallas_call(
        _scan_kernel,
        grid=(b, ns),
        in_specs=[
            pl.BlockSpec((None, c, 6 * BR_W), fwd),
            pl.BlockSpec((None, c, 6 * BR_W), bwd),
            pl.BlockSpec((None, c, LANES), fwd),
            pl.BlockSpec((None, c, LANES), bwd),
            pl.BlockSpec((2, c, N_HEADS * c), const3),
            pl.BlockSpec((2, c, BR_W), const3),
            pl.BlockSpec((2, c, BR_W), const3),
            pl.BlockSpec((1, BR_W), lambda bi, j: (0, 0)),
        ],
        out_specs=[
            pl.BlockSpec((None, c, 2 * BR_W), fwd),
            pl.BlockSpec((None, c, 2 * BR_W), bwd),
        ],
        out_shape=[
            jax.ShapeDtypeStruct((b, tt, 2 * BR_W), F32),
            jax.ShapeDtypeStruct((b, tt, 2 * BR_W), F32),
        ],
        scratch_shapes=[
            pltpu.VMEM((2, BR_W, BR_W), F32),
            pltpu.VMEM((2, BR_W, BR_W), F32),
        ],
        compiler_params=pltpu.CompilerParams(
            dimension_semantics=("arbitrary", "arbitrary"), vmem_limit_bytes=VMEM_LIMIT),
        name="bidirectional_scans",
    )(qkv, qkv, gates, gates, intra, qdec, kdec, cdec)


def _finish_kernel(of_ref, ob_ref, pf_ref, prev_ref, next_ref, x_ref, gate_ref, gpost_ref, wout_ref,
                   sgw_ref, sgb_ref, scw_ref, rng_ref, gng_ref, out_ref, y_ref,
                   *, n_blocks, n_ctx_blocks, skip):
    i = pl.program_id(1) + skip
    hm = _head_mean_matrix()
    o = of_ref[...] + ob_ref[...]
    pf = pf_ref

    o_ret = o[:, 0:BR_W]
    mu = _mm(o_ret, hm)
    cen = o_ret - mu
    var = _mm(cen * cen, hm)
    y_ref[:, 0:BR_W] = (cen * lax.rsqrt(var + EPS) * rng_ref[...]) * _silu(pf[:, 4 * BR_W:5 * BR_W])

    u = _gelu_tanh(pf[:, 5 * BR_W:6 * BR_W])
    v = _gelu_tanh(pf[:, 6 * BR_W:7 * BR_W])
    vm = jnp.mean(v, axis=-1, keepdims=True)
    vc = v - vm
    v = vc * lax.rsqrt(jnp.mean(vc * vc, axis=-1, keepdims=True) + EPS)
    sg_gate = _silu(pf[:, 7 * BR_W:8 * BR_W])
    sgw = sgw_ref[...]
    for n in range(v.shape[0] // RET_CHUNK):
        r0 = n * RET_CHUNK
        s = _mm(sgw, _blockdiag(v[r0:r0 + RET_CHUNK], RET_CHUNK)) + sgb_ref[...]
        y_ref[r0:r0 + RET_CHUNK, BR_W:2 * BR_W] = u[r0:r0 + RET_CHUNK] * s * sg_gate[r0:r0 + RET_CHUNK]

    prev_row, next_row = _halo_rows(prev_ref, next_ref, i, n_blocks, n_ctx_blocks)
    ch = pf[:, 0:BR_W] * pf[:, BR_W:2 * BR_W]
    conv = _conv3(ch, prev_row[:, 0:BR_W] * prev_row[:, BR_W:2 * BR_W],
                  next_row[:, 0:BR_W] * next_row[:, BR_W:2 * BR_W], scw_ref[...])
    y_ref[:, 2 * BR_W:3 * BR_W] = pf[:, 2 * BR_W:3 * BR_W] * conv * _silu(pf[:, 3 * BR_W:4 * BR_W])

    o_gdn = o[:, BR_W:2 * BR_W]
    ms = _mm(o_gdn * o_gdn, hm)
    y_ref[:, 3 * BR_W:4 * BR_W] = (o_gdn * lax.rsqrt(ms + EPS) * gng_ref[...]) * _silu(pf[:, 8 * BR_W:9 * BR_W])

    r = jnp.dot(y_ref[...].astype(BF16), wout_ref[...], preferred_element_type=F32)
    rn = (r * lax.rsqrt(jnp.mean(r * r, axis=-1, keepdims=True) + EPS)) * gpost_ref[...]
    out_ref[...] = x_ref[...] + gate_ref[...] * rn


def _finish(o_f, o_b, pf, xc, gate, g_post, w_out, sgw, sgb, sc_w, ret_g, gdn_g, *, tm, n_ctx_blocks, skip):
    b, tt, d = xc.shape
    nb = tt // tm
    hb = tm // SUBLANES
    kern = functools.partial(_finish_kernel, n_blocks=nb, n_ctx_blocks=n_ctx_blocks, skip=skip)
    row = lambda bi, i: (bi, i + skip, 0)
    const2 = lambda bi, i: (0, 0)
    return pl.pallas_call(
        kern,
        grid=(b, nb - skip),
        in_specs=[
            pl.BlockSpec((None, tm, 2 * BR_W), row),
            pl.BlockSpec((None, tm, 2 * BR_W), row),
            pl.BlockSpec((None, tm, FIN_W), row),
            pl.BlockSpec((None, SUBLANES, 2 * BR_W), lambda bi, i: (bi, jnp.maximum((i + skip) * hb - 1, 0), 0)),
            pl.BlockSpec((None, SUBLANES, 2 * BR_W),
                         lambda bi, i: (bi, jnp.minimum((i + skip + 1) * hb, tt // SUBLANES - 1), 0)),
            pl.BlockSpec((None, tm, d), row),
            pl.BlockSpec((None, None, 1, d), lambda bi, i: (bi, (i + skip >= n_ctx_blocks).astype(jnp.int32), 0, 0)),
            pl.BlockSpec((1, d), const2),
            pl.BlockSpec((4 * BR_W, d), const2),
            pl.BlockSpec((RET_CHUNK, N_HEADS * RET_CHUNK), const2),
            pl.BlockSpec((RET_CHUNK, BR_W), const2),
            pl.BlockSpec((3, BR_W), const2),
            pl.BlockSpec((1, BR_W), const2),
            pl.BlockSpec((1, BR_W), const2),
        ],
        out_specs=pl.BlockSpec((None, tm, d), lambda bi, i: (bi, i, 0)),
        out_shape=jax.ShapeDtypeStruct((b, tt - skip * tm, d), F32),
        scratch_shapes=[pltpu.VMEM((tm, 4 * BR_W), F32)],
        compiler_params=pltpu.CompilerParams(vmem_limit_bytes=VMEM_LIMIT),
        name="finish_and_out_projection",
    )(o_f, o_b, pf, pf, pf, xc, gate, g_post, w_out, sgw, sgb, sc_w, ret_g, gdn_g)


def _rope_tables(t_lat, t_ctx):
    nf = HEAD_DIM // 4
    inv = ROPE_BASE ** (-jnp.arange(nf, dtype=F32) / nf)
    rows = t_lat // GRID_W
    row = jnp.repeat(jnp.arange(rows), GRID_W).astype(F32)
    col = jnp.tile(jnp.arange(GRID_W), rows).astype(F32)
    ang_r = row[:, None] * inv
    ang_c = col[:, None] * inv
    ang = jnp.concatenate([ang_r, ang_r, ang_c, ang_c], axis=-1)
    cos = jnp.tile(jnp.cos(ang), (1, N_HEADS))
    sin = jnp.tile(jnp.sin(ang), (1, N_HEADS))
    cos = jnp.concatenate([jnp.ones((t_ctx, BR_W), F32), cos], axis=0)
    sin = jnp.concatenate([jnp.zeros((t_ctx, BR_W), F32), sin], axis=0)
    return cos, sin


def _retention_tables():
    c = RET_CHUNK
    log_gamma = jnp.log(1.0 - 2.0 ** (-5.0 - jnp.arange(N_HEADS, dtype=F32)))
    pos = jnp.arange(c, dtype=F32)
    lg = log_gamma[:, None]
    diff = pos[:, None] - pos[None, :]
    intra_f = jnp.exp(jnp.where(diff >= 0, diff * lg[..., None], -jnp.inf))
    intra_b = jnp.swapaxes(intra_f, 1, 2)
    wide = lambda m: jnp.transpose(m, (1, 0, 2)).reshape(c, N_HEADS * c)
    nat = lambda m: jnp.repeat(m.T, HEAD_DIM, axis=1)
    q_f = jnp.exp((pos + 1.0) * lg)
    k_f = jnp.exp((c - 1.0 - pos) * lg)
    intra = jnp.stack([wide(intra_f), wide(intra_b)])
    qdec = jnp.stack([nat(q_f), nat(q_f[:, ::-1])])
    kdec = jnp.stack([nat(k_f), nat(k_f[:, ::-1])])
    cdec = jnp.repeat(jnp.exp(c * log_gamma), HEAD_DIM)[None, :]
    return intra, qdec, kdec, cdec


def _reorder_w_in(w_in):
    depth, d, _ = w_in.shape
    seg = lambda n: w_in[:, :, n * BR_W:(n + 1) * BR_W]
    gates = w_in[:, :, 15 * BR_W:]
    pad = jnp.zeros((depth, d, LANES - N_GATES), w_in.dtype)
    order = [seg(0), seg(1), seg(2), seg(11), seg(12), seg(13), gates, pad,
             seg(8), seg(9), seg(7), seg(10), seg(3), seg(4), seg(5), seg(6), seg(14)]
    return jnp.concatenate(order, axis=-1).astype(BF16)


def kernel(x, c, ctx, c_ctx, w_mod, b_mod, g_pre, g_post, w_in, w_out, ret_norm_g, sg_w, sg_b,
           sc_conv_w, gdn_conv_w, gdn_a_log, gdn_dt_bias, gdn_norm_g):
    b, t_lat, d = x.shape
    t_ctx = ctx.shape[1]
    depth = w_mod.shape[0]
    tm = 256
    assert t_ctx % tm == 0 and t_lat % tm == 0 and tm % RET_CHUNK == 0 and t_lat % GRID_W == 0
    n_ctx_blocks = t_ctx // tm

    bp = -(-(b + 1) // SUBLANES) * SUBLANES
    cc = jnp.concatenate([c, c_ctx[None, :], jnp.zeros((bp - b - 1, d), F32)], axis=0)
    mod = _modulation(cc, w_mod, b_mod)
    mod = jnp.stack([jnp.broadcast_to(mod[:, :, b:b + 1], (depth, 3, b, d)), mod[:, :, :b]], axis=3)
    mod = mod[:, :, :, :, None, :]

    cos_tab, sin_tab = _rope_tables(t_lat, t_ctx)
    intra, qdec, kdec, cdec = _retention_tables()
    w_in_r = _reorder_w_in(w_in)
    w_out_b = w_out.astype(BF16)
    prm = jnp.zeros((depth, SUBLANES, LANES), F32)
    prm = prm.at[:, 0, :N_GATES // 2].set(gdn_a_log.reshape(depth, -1))
    prm = prm.at[:, 1, :N_GATES // 2].set(gdn_dt_bias.reshape(depth, -1))
    sgw = jnp.transpose(sg_w, (0, 2, 1, 3)).reshape(depth, RET_CHUNK, N_HEADS * RET_CHUNK)
    sgb = jnp.repeat(jnp.swapaxes(sg_b, 1, 2), HEAD_DIM, axis=2)
    gdn_g = jnp.tile(gdn_norm_g, (1, N_HEADS))

    xc = jnp.concatenate([ctx, x], axis=1)
    for l in range(depth):
        last = l == depth - 1
        ps, pf = _in_projection(xc, mod[l, 0], mod[l, 1], g_pre[l][None, :], w_in_r[l],
                                tm=tm, n_ctx_blocks=n_ctx_blocks)
        qkv, gates = _prepare(ps, cos_tab, sin_tab, gdn_conv_w[l], prm[l], tb=tm, n_ctx_blocks=n_ctx_blocks)
        o_f, o_b = _scans(qkv, gates, intra, qdec, kdec, cdec, n_ctx_steps=t_ctx // RET_CHUNK)
        xc = _finish(o_f, o_b, pf, xc, mod[l, 2], g_post[l][None, :], w_out_b[l], sgw[l], sgb[l],
                     sc_conv_w[l], ret_norm_g[l][None, :], gdn_g[l][None, :],
                     tm=tm, n_ctx_blocks=n_ctx_blocks, skip=n_ctx_blocks if last else 0)
    return xc
```

```python
import functools

import jax
import jax.numpy as jnp
from jax import lax
from jax.experimental import pallas as pl
from jax.experimental.pallas import tpu as pltpu

HEAD_DIM = 64
N_HEADS = 4
BR_W = N_HEADS * HEAD_DIM
GRID_W = 64
RET_CHUNK = 128
GDN_CHUNK = 64
ROPE_BASE = 10000.0
EPS = 1e-6
N_GATES = 16
LANES = 128
SUBLANES = 8
SCAN_W = 6 * BR_W + LANES
FIN_W = 9 * BR_W
VMEM_LIMIT = 56 * 1024 * 1024

F32 = jnp.float32
BF16 = jnp.bfloat16
NEG_INF = float("-inf")


def _mm(a, b):
    return jnp.dot(a.astype(BF16), b.astype(BF16), preferred_element_type=F32)


def _mm_nt(a, b):
    return lax.dot_general(a.astype(BF16), b.astype(BF16), (((1,), (1,)), ((), ())),
                           preferred_element_type=F32)


def _mm_tn(a, b):
    return lax.dot_general(a.astype(BF16), b.astype(BF16), (((0,), (0,)), ((), ())),
                           preferred_element_type=F32)


def _iota(shape, dim):
    return lax.broadcasted_iota(jnp.int32, shape, dim)


def _silu(x):
    return x * jax.nn.sigmoid(x)


def _gelu_tanh(x):
    return 0.5 * x * (1.0 + jnp.tanh(0.7978845608028654 * (x + 0.044715 * (x * x * x))))


def _head_mean_matrix():
    r = _iota((BR_W, BR_W), 0) >> 6
    c = _iota((BR_W, BR_W), 1) >> 6
    return jnp.where(r == c, 1.0 / HEAD_DIM, 0.0).astype(F32)


def _blockdiag(x, chunk):
    shift = chunk.bit_length() - 1
    t = jnp.concatenate([x] * N_HEADS, axis=0)
    keep = (_iota(t.shape, 0) >> shift) == (_iota(t.shape, 1) >> 6)
    return jnp.where(keep, t, 0.0)


def _mod_kernel(cc_ref, w_ref, b_ref, o_ref):
    s = _silu(cc_ref[...])
    o_ref[...] = _mm(s, w_ref[...]) + b_ref[...]


def _modulation(cc, w_mod, b_mod):
    depth, d, _ = w_mod.shape
    bp = cc.shape[0]
    return pl.pallas_call(
        _mod_kernel,
        grid=(depth, 3),
        in_specs=[
            pl.BlockSpec((bp, d), lambda l, n: (0, 0)),
            pl.BlockSpec((None, d, d), lambda l, n: (l, 0, n)),
            pl.BlockSpec((None, None, 1, d), lambda l, n: (l, n, 0, 0)),
        ],
        out_specs=pl.BlockSpec((None, None, bp, d), lambda l, n: (l, n, 0, 0)),
        out_shape=jax.ShapeDtypeStruct((depth, 3, bp, d), F32),
        compiler_params=pltpu.CompilerParams(vmem_limit_bytes=VMEM_LIMIT),
        name="modulation",
    )(cc, w_mod, b_mod.reshape(depth, 3, 1, d))


def _inproj_kernel(x_ref, shift_ref, scale_ref, g_ref, w_ref, ps_ref, pf_ref):
    x = x_ref[...]
    y = x * lax.rsqrt(jnp.mean(x * x, axis=-1, keepdims=True) + EPS)
    h = (y * g_ref[...]) * (1.0 + scale_ref[...]) + shift_ref[...]
    hb = h.astype(BF16)
    ps_ref[...] = jnp.dot(hb, w_ref[:, :SCAN_W], preferred_element_type=F32)
    pf_ref[...] = jnp.dot(hb, w_ref[:, SCAN_W:], preferred_element_type=F32)


def _in_projection(xc, shift, scale, g_pre, w_in, *, tm, n_ctx_blocks):
    b, tt, d = xc.shape
    sel = lambda bi, i: (bi, (i >= n_ctx_blocks).astype(jnp.int32), 0, 0)
    return pl.pallas_call(
        _inproj_kernel,
        grid=(b, tt // tm),
        in_specs=[
            pl.BlockSpec((None, tm, d), lambda bi, i: (bi, i, 0)),
            pl.BlockSpec((None, None, 1, d), sel),
            pl.BlockSpec((None, None, 1, d), sel),
            pl.BlockSpec((1, d), lambda bi, i: (0, 0)),
            pl.BlockSpec((d, SCAN_W + FIN_W), lambda bi, i: (0, 0)),
        ],
        out_specs=[
            pl.BlockSpec((None, tm, SCAN_W), lambda bi, i: (bi, i, 0)),
            pl.BlockSpec((None, tm, FIN_W), lambda bi, i: (bi, i, 0)),
        ],
        out_shape=[
            jax.ShapeDtypeStruct((b, tt, SCAN_W), F32),
            jax.ShapeDtypeStruct((b, tt, FIN_W), F32),
        ],
        compiler_params=pltpu.CompilerParams(vmem_limit_bytes=VMEM_LIMIT),
        name="in_projection",
    )(xc, shift, scale, g_pre, w_in)


def _conv3(x, prev_row, next_row, w):
    t = x.shape[0]
    row = _iota(x.shape, 0)
    x_prev = jnp.where(row == 0, prev_row, pltpu.roll(x, 1, axis=0))
    x_next = jnp.where(row == t - 1, next_row, pltpu.roll(x, t - 1, axis=0))
    return x_prev * w[0:1, :] + x * w[1:2, :] + x_next * w[2:3, :]


def _halo_rows(prev_ref, next_ref, i, n_blocks, n_ctx_blocks):
    has_prev = jnp.logical_and(i > 0, i != n_ctx_blocks)
    has_next = jnp.logical_and(i < n_blocks - 1, i != n_ctx_blocks - 1)
    prev_row = jnp.where(has_prev, prev_ref[SUBLANES - 1:SUBLANES, :], 0.0)
    next_row = jnp.where(has_next, next_ref[0:1, :], 0.0)
    return prev_row, next_row


def _prep_kernel(ps_ref, prev_ref, next_ref, cos_ref, sin_ref, cw_ref, prm_ref,
                 qkv_ref, gate_ref, *, n_blocks, n_ctx_blocks):
    i = pl.program_id(1)
    ps = ps_ref[...]
    cos = cos_ref[...]
    sin = sin_ref[...]
    lane = _iota(cos.shape, 1)
    first_half = (lane & 31) < 16

    def rope(x):
        partner = jnp.where(first_half, -pltpu.roll(x, BR_W - 16, axis=1), pltpu.roll(x, 16, axis=1))
        return x * cos + partner * sin

    qkv_ref[:, 0 * BR_W:1 * BR_W] = rope(ps[:, 0 * BR_W:1 * BR_W]).astype(BF16)
    qkv_ref[:, 1 * BR_W:2 * BR_W] = (rope(ps[:, 1 * BR_W:2 * BR_W]) * HEAD_DIM ** -0.5).astype(BF16)
    qkv_ref[:, 2 * BR_W:3 * BR_W] = ps[:, 2 * BR_W:3 * BR_W].astype(BF16)

    prev_row, next_row = _halo_rows(prev_ref, next_ref, i, n_blocks, n_ctx_blocks)
    g = _silu(_conv3(ps[:, 3 * BR_W:6 * BR_W], prev_row, next_row, cw_ref[...]))
    hm = _head_mean_matrix() * HEAD_DIM

    def l2n(x):
        return x * lax.rsqrt(_mm(x * x, hm) + EPS)

    qkv_ref[:, 3 * BR_W:4 * BR_W] = (l2n(g[:, 0:BR_W]) * HEAD_DIM ** -0.5).astype(BF16)
    qkv_ref[:, 4 * BR_W:5 * BR_W] = l2n(g[:, BR_W:2 * BR_W]).astype(BF16)
    qkv_ref[:, 5 * BR_W:6 * BR_W] = g[:, 2 * BR_W:3 * BR_W].astype(BF16)

    a = ps[:, 6 * BR_W:]
    neg_rate = -jnp.exp(prm_ref[0:1, :])
    z = a + prm_ref[1:2, :]
    softplus = jnp.maximum(z, 0.0) + jnp.log1p(jnp.exp(-jnp.abs(z)))
    glane = _iota(a.shape, 1)
    gate_ref[...] = jnp.where(glane < N_GATES // 2, neg_rate * softplus, jax.nn.sigmoid(a))


def _prepare(ps, cos_tab, sin_tab, conv_w, prm, *, tb, n_ctx_blocks):
    b, tt, _ = ps.shape
    nb = tt // tb
    hb = tb // SUBLANES
    kern = functools.partial(_prep_kernel, n_blocks=nb, n_ctx_blocks=n_ctx_blocks)
    return pl.pallas_call(
        kern,
        grid=(b, nb),
        in_specs=[
            pl.BlockSpec((None, tb, SCAN_W), lambda bi, i: (bi, i, 0)),
            pl.BlockSpec((None, SUBLANES, 3 * BR_W), lambda bi, i: (bi, jnp.maximum(i * hb - 1, 0), 1)),
            pl.BlockSpec((None, SUBLANES, 3 * BR_W),
                         lambda bi, i: (bi, jnp.minimum((i + 1) * hb, tt // SUBLANES - 1), 1)),
            pl.BlockSpec((tb, BR_W), lambda bi, i: (i, 0)),
            pl.BlockSpec((tb, BR_W), lambda bi, i: (i, 0)),
            pl.BlockSpec((3, 3 * BR_W), lambda bi, i: (0, 0)),
            pl.BlockSpec((SUBLANES, LANES), lambda bi, i: (0, 0)),
        ],
        out_specs=[
            pl.BlockSpec((None, tb, 6 * BR_W), lambda bi, i: (bi, i, 0)),
            pl.BlockSpec((None, tb, LANES), lambda bi, i: (bi, i, 0)),
        ],
        out_shape=[
            jax.ShapeDtypeStruct((b, tt, 6 * BR_W), BF16),
            jax.ShapeDtypeStruct((b, tt, LANES), F32),
        ],
        compiler_params=pltpu.CompilerParams(vmem_limit_bytes=VMEM_LIMIT),
        name="prepare_scan_inputs",
    )(ps, ps, ps, cos_tab, sin_tab, conv_w, prm)


def _chunk_cumsum(x, chunk, reverse):
    t = x.shape[0]
    pos = _iota(x.shape, 0) & (chunk - 1)
    s = 1
    while s < chunk:
        if reverse:
            x = x + jnp.where(pos < chunk - s, pltpu.roll(x, t - s, axis=0), 0.0)
        else:
            x = x + jnp.where(pos >= s, pltpu.roll(x, s, axis=0), 0.0)
        s *= 2
    return x


def _expand_exact(x, e):
    hi = x.astype(BF16)
    r1 = x - hi.astype(F32)
    mid = r1.astype(BF16)
    lo = (r1 - mid.astype(F32)).astype(BF16)
    eb = e.astype(BF16)
    return (jnp.dot(hi, eb, preferred_element_type=F32) + jnp.dot(mid, eb, preferred_element_type=F32)
            + jnp.dot(lo, eb, preferred_element_type=F32))


def _interleave(chains):
    results = [None] * len(chains)
    live = list(enumerate(chains))
    while live:
        still = []
        for idx, chain in live:
            try:
                next(chain)
                still.append((idx, chain))
            except StopIteration as stop:
                results[idx] = stop.value
        live = still
    return results


def _retention_chain(q, k, v, s_ref, o_ref, intra, qdec, kdec, cdec, bd_mask):
    scores = _mm_nt(q, _blockdiag(k, RET_CHUNK)) * intra
    yield
    s = s_ref[...]
    o_ref[:, 0:BR_W] = _mm(scores, _blockdiag(v, RET_CHUNK)) + _mm(q, s) * qdec
    yield
    s_ref[...] = s * cdec + jnp.where(bd_mask, _mm_tn(k * kdec, v), 0.0)


def _unit_triangular_inverse(a, row, col, eye):
    c = GDN_CHUNK
    same16 = (row >> 4) == (col >> 4)
    same32 = (row >> 5) == (col >> 5)
    a16 = jnp.where(same16, a, 0.0)
    p = jnp.where(eye, 1.0, 0.0) - a16
    x = _mm(a16, _blockdiag(a16, c))
    yield
    for _ in range(2):
        px = _mm(jnp.concatenate([p, x], axis=0), _blockdiag(x, c))
        yield
        p = p + px[:c]
        x = px[c:]
    p = p + _mm(p, _blockdiag(x, c))
    yield
    for off in (jnp.where(jnp.logical_and(same32, jnp.logical_not(same16)), a, 0.0),
                jnp.where(same32, 0.0, a)):
        y = _mm(p, _blockdiag(off, c))
        yield
        p = p - _mm(y, _blockdiag(p, c))
        yield
    return p


def _gdn_chunk_operands(q, k, v, gc, beta, reverse):
    c = GDN_CHUNK
    row = _iota((c, BR_W), 0)
    col = _iota((c, BR_W), 1) & (c - 1)
    eye = row == col
    incl = (row <= col) if reverse else (row >= col)
    strict = (row < col) if reverse else (row > col)
    gc_row = jnp.sum(jnp.where(eye, gc, 0.0), axis=0, keepdims=True)
    decay = jnp.exp(jnp.where(incl, gc - gc_row, NEG_INF))
    kk_qk = _mm_nt(jnp.concatenate([k, q], axis=0), _blockdiag(k, c))
    yield
    a = jnp.where(strict, kk_qk[:c] * beta * decay, 0.0)
    inv = yield from _unit_triangular_inverse(a, row, col, eye)
    egc = jnp.exp(gc)
    u = _mm(inv, _blockdiag(v * beta, c))
    w = _mm(inv, _blockdiag(k * beta * egc, c))
    yield
    g_last = gc[0:1, :] if reverse else gc[c - 1:c, :]
    return dict(u=u, w=w, qd=q * egc, intra=kk_qk[c:] * decay, k_tail=k * jnp.exp(g_last - gc),
                cdec=jnp.exp(g_last))


def _gdn_recurrence_chain(chunks, s_ref, bd_mask):
    c = GDN_CHUNK
    s = s_ref[...]
    for ch, o_ref in chunks:
        ws_qs = _mm(jnp.concatenate([ch["w"], ch["qd"]], axis=0), s)
        yield
        v_new = ch["u"] - ws_qs[:c]
        o_ref[...] = ws_qs[c:] + _mm(ch["intra"], _blockdiag(v_new, c))
        s = s * ch["cdec"] + jnp.where(bd_mask, _mm_tn(ch["k_tail"], v_new), 0.0)
        yield
    s_ref[...] = s


def _scan_kernel(qkv_f_ref, qkv_b_ref, gate_f_ref, gate_b_ref, intra_ref, qdec_ref, kdec_ref,
                 cdec_ref, o_f_ref, o_b_ref, s_ret_ref, s_gdn_ref):
    j = pl.program_id(1)

    @pl.when(j == 0)
    def _():
        s_ret_ref[...] = jnp.zeros_like(s_ret_ref)
        s_gdn_ref[...] = jnp.zeros_like(s_gdn_ref)

    bd_mask = (_iota((BR_W, BR_W), 0) >> 6) == (_iota((BR_W, BR_W), 1) >> 6)
    erow = _iota((LANES, BR_W), 0)
    ehead = _iota((LANES, BR_W), 1) >> 6
    c = GDN_CHUNK
    n_sub = RET_CHUNK // c

    chains, chunk_refs = [], []
    for d, (qkv_ref, gate_ref, o_ref) in enumerate(((qkv_f_ref, gate_f_ref, o_f_ref),
                                                    (qkv_b_ref, gate_b_ref, o_b_ref))):
        reverse = d == 1
        qkv = qkv_ref[...].astype(F32)
        chains.append(_retention_chain(qkv[:, 0:BR_W], qkv[:, BR_W:2 * BR_W], qkv[:, 2 * BR_W:3 * BR_W],
                                       s_ret_ref.at[d], o_ref, intra_ref[d], qdec_ref[d], kdec_ref[d],
                                       cdec_ref[...], bd_mask))
        gates = gate_ref[...]
        gc_all = _expand_exact(_chunk_cumsum(gates, c, reverse),
                               jnp.where(erow == N_HEADS * d + ehead, 1.0, 0.0))
        beta_all = _mm(gates, jnp.where(erow == N_GATES // 2 + N_HEADS * d + ehead, 1.0, 0.0))
        for sub in (range(n_sub - 1, -1, -1) if reverse else range(n_sub)):
            r0 = sub * c
            chains.append(_gdn_chunk_operands(
                qkv[r0:r0 + c, 3 * BR_W:4 * BR_W], qkv[r0:r0 + c, 4 * BR_W:5 * BR_W],
                qkv[r0:r0 + c, 5 * BR_W:6 * BR_W], gc_all[r0:r0 + c], beta_all[r0:r0 + c], reverse))
            chunk_refs.append(o_ref.at[r0:r0 + c, BR_W:2 * BR_W])

    operands = [r for r in _interleave(chains) if r is not None]
    _interleave([_gdn_recurrence_chain(list(zip(operands[d * n_sub:(d + 1) * n_sub],
                                                chunk_refs[d * n_sub:(d + 1) * n_sub])),
                                       s_gdn_ref.at[d], bd_mask) for d in range(2)])


def _bwd_block(j, n_steps, n_ctx_steps):
    return jnp.where(j < n_ctx_steps, n_ctx_steps - 1 - j, n_steps - 1 - (j - n_ctx_steps))


def _scans(qkv, gates, intra, qdec, kdec, cdec, *, n_ctx_steps):
    b, tt, _ = qkv.shape
    c = RET_CHUNK
    ns = tt // c
    fwd = lambda bi, j: (bi, j, 0)
    bwd = lambda bi, j: (bi, _bwd_block(j, ns, n_ctx_steps), 0)
    const3 = lambda bi, j: (0, 0, 0)
    return pl.pallas_call(
        _scan_kernel,
        grid=(b, ns),
        in_specs=[
            pl.BlockSpec((None, c, 6 * BR_W), fwd),
            pl.BlockSpec((None, c, 6 * BR_W), bwd),
            pl.BlockSpec((None, c, LANES), fwd),
            pl.BlockSpec((None, c, LANES), bwd),
            pl.BlockSpec((2, c, N_HEADS * c), const3),
            pl.BlockSpec((2, c, BR_W), const3),
            pl.BlockSpec((2, c, BR_W), const3),
            pl.BlockSpec((1, BR_W), lambda bi, j: (0, 0)),
        ],
        out_specs=[
            pl.BlockSpec((None, c, 2 * BR_W), fwd),
            pl.BlockSpec((None, c, 2 * BR_W), bwd),
        ],
        out_shape=[
            jax.ShapeDtypeStruct((b, tt, 2 * BR_W), F32),
            jax.ShapeDtypeStruct((b, tt, 2 * BR_W), F32),
        ],
        scratch_shapes=[
            pltpu.VMEM((2, BR_W, BR_W), F32),
            pltpu.VMEM((2, BR_W, BR_W), F32),
        ],
        compiler_params=pltpu.CompilerParams(
            dimension_semantics=("arbitrary", "arbitrary"), vmem_limit_bytes=VMEM_LIMIT),
        name="bidirectional_scans",
    )(qkv, qkv, gates, gates, intra, qdec, kdec, cdec)


def _finish_kernel(of_ref, ob_ref, pf_ref, prev_ref, next_ref, x_ref, gate_ref, gpost_ref, wout_ref,
                   sgw_ref, sgb_ref, scw_ref, rng_ref, gng_ref, out_ref, y_ref,
                   *, n_blocks, n_ctx_blocks, skip):
    i = pl.program_id(1) + skip
    hm = _head_mean_matrix()
    o = of_ref[...] + ob_ref[...]
    pf = pf_ref

    o_ret = o[:, 0:BR_W]
    mu = _mm(o_ret, hm)
    cen = o_ret - mu
    var = _mm(cen * cen, hm)
    y_ref[:, 0:BR_W] = (cen * lax.rsqrt(var + EPS) * rng_ref[...]) * _silu(pf[:, 4 * BR_W:5 * BR_W])

    u = _gelu_tanh(pf[:, 5 * BR_W:6 * BR_W])
    v = _gelu_tanh(pf[:, 6 * BR_W:7 * BR_W])
    vm = jnp.mean(v, axis=-1, keepdims=True)
    vc = v - vm
    v = vc * lax.rsqrt(jnp.mean(vc * vc, axis=-1, keepdims=True) + EPS)
    sg_gate = _silu(pf[:, 7 * BR_W:8 * BR_W])
    sgw = sgw_ref[...]
    for n in range(v.shape[0] // RET_CHUNK):
        r0 = n * RET_CHUNK
        s = _mm(sgw, _blockdiag(v[r0:r0 + RET_CHUNK], RET_CHUNK)) + sgb_ref[...]
        y_ref[r0:r0 + RET_CHUNK, BR_W:2 * BR_W] = u[r0:r0 + RET_CHUNK] * s * sg_gate[r0:r0 + RET_CHUNK]

    prev_row, next_row = _halo_rows(prev_ref, next_ref, i, n_blocks, n_ctx_blocks)
    ch = pf[:, 0:BR_W] * pf[:, BR_W:2 * BR_W]
    conv = _conv3(ch, prev_row[:, 0:BR_W] * prev_row[:, BR_W:2 * BR_W],
                  next_row[:, 0:BR_W] * next_row[:, BR_W:2 * BR_W], scw_ref[...])
    y_ref[:, 2 * BR_W:3 * BR_W] = pf[:, 2 * BR_W:3 * BR_W] * conv * _silu(pf[:, 3 * BR_W:4 * BR_W])

    o_gdn = o[:, BR_W:2 * BR_W]
    ms = _mm(o_gdn * o_gdn, hm)
    y_ref[:, 3 * BR_W:4 * BR_W] = (o_gdn * lax.rsqrt(ms + EPS) * gng_ref[...]) * _silu(pf[:, 8 * BR_W:9 * BR_W])

    r = jnp.dot(y_ref[...].astype(BF16), wout_ref[...], preferred_element_type=F32)
    rn = (r * lax.rsqrt(jnp.mean(r * r, axis=-1, keepdims=True) + EPS)) * gpost_ref[...]
    out_ref[...] = x_ref[...] + gate_ref[...] * rn


def _finish(o_f, o_b, pf, xc, gate, g_post, w_out, sgw, sgb, sc_w, ret_g, gdn_g, *, tm, n_ctx_blocks, skip):
    b, tt, d = xc.shape
    nb = tt // tm
    hb = tm // SUBLANES
    kern = functools.partial(_finish_kernel, n_blocks=nb, n_ctx_blocks=n_ctx_blocks, skip=skip)
    row = lambda bi, i: (bi, i + skip, 0)
    const2 = lambda bi, i: (0, 0)
    return pl.pallas_call(
        kern,
        grid=(b, nb - skip),
        in_specs=[
            pl.BlockSpec((None, tm, 2 * BR_W), row),
            pl.BlockSpec((None, tm, 2 * BR_W), row),
            pl.BlockSpec((None, tm, FIN_W), row),
            pl.BlockSpec((None, SUBLANES, 2 * BR_W), lambda bi, i: (bi, jnp.maximum((i + skip) * hb - 1, 0), 0)),
            pl.BlockSpec((None, SUBLANES, 2 * BR_W),
                         lambda bi, i: (bi, jnp.minimum((i + skip + 1) * hb, tt // SUBLANES - 1), 0)),
            pl.BlockSpec((None, tm, d), row),
            pl.BlockSpec((None, None, 1, d), lambda bi, i: (bi, (i + skip >= n_ctx_blocks).astype(jnp.int32), 0, 0)),
            pl.BlockSpec((1, d), const2),
            pl.BlockSpec((4 * BR_W, d), const2),
            pl.BlockSpec((RET_CHUNK, N_HEADS * RET_CHUNK), const2),
            pl.BlockSpec((RET_CHUNK, BR_W), const2),
            pl.BlockSpec((3, BR_W), const2),
            pl.BlockSpec((1, BR_W), const2),
            pl.BlockSpec((1, BR_W), const2),
        ],
        out_specs=pl.BlockSpec((None, tm, d), lambda bi, i: (bi, i, 0)),
        out_shape=jax.ShapeDtypeStruct((b, tt - skip * tm, d), F32),
        scratch_shapes=[pltpu.VMEM((tm, 4 * BR_W), F32)],
        compiler_params=pltpu.CompilerParams(vmem_limit_bytes=VMEM_LIMIT),
        name="finish_and_out_projection",
    )(o_f, o_b, pf, pf, pf, xc, gate, g_post, w_out, sgw, sgb, sc_w, ret_g, gdn_g)


def _rope_tables(t_lat, t_ctx):
    nf = HEAD_DIM // 4
    inv = ROPE_BASE ** (-jnp.arange(nf, dtype=F32) / nf)
    rows = t_lat // GRID_W
    row = jnp.repeat(jnp.arange(rows), GRID_W).astype(F32)
    col = jnp.tile(jnp.arange(GRID_W), rows).astype(F32)
    ang_r = row[:, None] * inv
    ang_c = col[:, None] * inv
    ang = jnp.concatenate([ang_r, ang_r, ang_c, ang_c], axis=-1)
    cos = jnp.tile(jnp.cos(ang), (1, N_HEADS))
    sin = jnp.tile(jnp.sin(ang), (1, N_HEADS))
    cos = jnp.concatenate([jnp.ones((t_ctx, BR_W), F32), cos], axis=0)
    sin = jnp.concatenate([jnp.zeros((t_ctx, BR_W), F32), sin], axis=0)
    return cos, sin


def _retention_tables():
    c = RET_CHUNK
    log_gamma = jnp.log(1.0 - 2.0 ** (-5.0 - jnp.arange(N_HEADS, dtype=F32)))
    pos = jnp.arange(c, dtype=F32)
    lg = log_gamma[:, None]
    diff = pos[:, None] - pos[None, :]
    intra_f = jnp.exp(jnp.where(diff >= 0, diff * lg[..., None], -jnp.inf))
    intra_b = jnp.swapaxes(intra_f, 1, 2)
    wide = lambda m: jnp.transpose(m, (1, 0, 2)).reshape(c, N_HEADS * c)
    nat = lambda m: jnp.repeat(m.T, HEAD_DIM, axis=1)
    q_f = jnp.exp((pos + 1.0) * lg)
    k_f = jnp.exp((c - 1.0 - pos) * lg)
    intra = jnp.stack([wide(intra_f), wide(intra_b)])
    qdec = jnp.stack([nat(q_f), nat(q_f[:, ::-1])])
    kdec = jnp.stack([nat(k_f), nat(k_f[:, ::-1])])
    cdec = jnp.repeat(jnp.exp(c * log_gamma), HEAD_DIM)[None, :]
    return intra, qdec, kdec, cdec


def _reorder_w_in(w_in):
    depth, d, _ = w_in.shape
    seg = lambda n: w_in[:, :, n * BR_W:(n + 1) * BR_W]
    gates = w_in[:, :, 15 * BR_W:]
    pad = jnp.zeros((depth, d, LANES - N_GATES), w_in.dtype)
    order = [seg(0), seg(1), seg(2), seg(11), seg(12), seg(13), gates, pad,
             seg(8), seg(9), seg(7), seg(10), seg(3), seg(4), seg(5), seg(6), seg(14)]
    return jnp.concatenate(order, axis=-1).astype(BF16)


def kernel(x, c, ctx, c_ctx, w_mod, b_mod, g_pre, g_post, w_in, w_out, ret_norm_g, sg_w, sg_b,
           sc_conv_w, gdn_conv_w, gdn_a_log, gdn_dt_bias, gdn_norm_g):
    b, t_lat, d = x.shape
    t_ctx = ctx.shape[1]
    depth = w_mod.shape[0]
    tm = 256
    assert t_ctx % tm == 0 and t_lat % tm == 0 and tm % RET_CHUNK == 0 and t_lat % GRID_W == 0
    n_ctx_blocks = t_ctx // tm

    bp = -(-(b + 1) // SUBLANES) * SUBLANES
    cc = jnp.concatenate([c, c_ctx[None, :], jnp.zeros((bp - b - 1, d), F32)], axis=0)
    mod = _modulation(cc, w_mod, b_mod)
    mod = jnp.stack([jnp.broadcast_to(mod[:, :, b:b + 1], (depth, 3, b, d)), mod[:, :, :b]], axis=3)
    mod = mod[:, :, :, :, None, :]

    cos_tab, sin_tab = _rope_tables(t_lat, t_ctx)
    intra, qdec, kdec, cdec = _retention_tables()
    w_in_r = _reorder_w_in(w_in)
    w_out_b = w_out.astype(BF16)
    prm = jnp.zeros((depth, SUBLANES, LANES), F32)
    prm = prm.at[:, 0, :N_GATES // 2].set(gdn_a_log.reshape(depth, -1))
    prm = prm.at[:, 1, :N_GATES // 2].set(gdn_dt_bias.reshape(depth, -1))
    sgw = jnp.transpose(sg_w, (0, 2, 1, 3)).reshape(depth, RET_CHUNK, N_HEADS * RET_CHUNK)
    sgb = jnp.repeat(jnp.swapaxes(sg_b, 1, 2), HEAD_DIM, axis=2)
    gdn_g = jnp.tile(gdn_norm_g, (1, N_HEADS))

    xc = jnp.concatenate([ctx, x], axis=1)
    for l in range(depth):
        last = l == depth - 1
        ps, pf = _in_projection(xc, mod[l, 0], mod[l, 1], g_pre[l][None, :], w_in_r[l],
                                tm=tm, n_ctx_blocks=n_ctx_blocks)
        qkv, gates = _prepare(ps, cos_tab, sin_tab, gdn_conv_w[l], prm[l], tb=tm, n_ctx_blocks=n_ctx_blocks)
        o_f, o_b = _scans(qkv, gates, intra, qdec, kdec, cdec, n_ctx_steps=t_ctx // RET_CHUNK)
        xc = _finish(o_f, o_b, pf, xc, mod[l, 2], g_post[l][None, :], w_out_b[l], sgw[l], sgb[l],
                     sc_conv_w[l], ret_norm_g[l][None, :], gdn_g[l][None, :],
                     tm=tm, n_ctx_blocks=n_ctx_blocks, skip=n_ctx_blocks if last else 0)
    return xc
```

```python
import functools

import jax
import jax.numpy as jnp
from jax import lax
from jax.experimental import pallas as pl
from jax.experimental.pallas import tpu as pltpu

HEAD_DIM = 64
N_HEADS = 4
BR_W = N_HEADS * HEAD_DIM
GRID_W = 64
RET_CHUNK = 128
GDN_CHUNK = 64
ROPE_BASE = 10000.0
EPS = 1e-6
N_GATES = 16
LANES = 128
SUBLANES = 8
SCAN_W = 6 * BR_W + LANES
FIN_W = 9 * BR_W
VMEM_LIMIT = 56 * 1024 * 1024

F32 = jnp.float32
BF16 = jnp.bfloat16
NEG_INF = float("-inf")


def _mm(a, b):
    return jnp.dot(a.astype(BF16), b.astype(BF16), preferred_element_type=F32)


def _mm_nt(a, b):
    return lax.dot_general(a.astype(BF16), b.astype(BF16), (((1,), (1,)), ((), ())),
                           preferred_element_type=F32)


def _mm_tn(a, b):
    return lax.dot_general(a.astype(BF16), b.astype(BF16), (((0,), (0,)), ((), ())),
                           preferred_element_type=F32)


def _iota(shape, dim):
    return lax.broadcasted_iota(jnp.int32, shape, dim)


def _silu(x):
    return x * jax.nn.sigmoid(x)


def _gelu_tanh(x):
    return 0.5 * x * (1.0 + jnp.tanh(0.7978845608028654 * (x + 0.044715 * (x * x * x))))


def _head_mean_matrix():
    r = _iota((BR_W, BR_W), 0) >> 6
    c = _iota((BR_W, BR_W), 1) >> 6
    return jnp.where(r == c, 1.0 / HEAD_DIM, 0.0).astype(F32)


def _blockdiag(x, chunk):
    shift = chunk.bit_length() - 1
    t = jnp.concatenate([x] * N_HEADS, axis=0)
    keep = (_iota(t.shape, 0) >> shift) == (_iota(t.shape, 1) >> 6)
    return jnp.where(keep, t, 0.0)


def _modulated_norm(x_ref, prev_ref, next_ref, g_ref, scale_ref, shift_ref):
    x = jnp.concatenate([x_ref[...], prev_ref[...], next_ref[...]], axis=0)
    y = x * lax.rsqrt(jnp.mean(x * x, axis=-1, keepdims=True) + EPS)
    return ((y * g_ref[...]) * (1.0 + scale_ref[...]) + shift_ref[...]).astype(BF16)


def _conv3(x, prev_row, next_row, w):
    t = x.shape[0]
    row = _iota(x.shape, 0)
    x_prev = jnp.where(row == 0, prev_row, pltpu.roll(x, 1, axis=0))
    x_next = jnp.where(row == t - 1, next_row, pltpu.roll(x, t - 1, axis=0))
    return x_prev * w[0:1, :] + x * w[1:2, :] + x_next * w[2:3, :]


def _halo_rows(ext, tm, i, n_blocks, n_ctx_blocks):
    has_prev = jnp.logical_and(i > 0, i != n_ctx_blocks)
    has_next = jnp.logical_and(i < n_blocks - 1, i != n_ctx_blocks - 1)
    prev_row = jnp.where(has_prev, ext[tm + SUBLANES - 1:tm + SUBLANES, :], 0.0)
    next_row = jnp.where(has_next, ext[tm + SUBLANES:tm + SUBLANES + 1, :], 0.0)
    return ext[:tm], prev_row, next_row


def _token_tile_specs(tm, d, tt, n_ctx_blocks, skip=0):
    hb = tm // SUBLANES
    tile = pl.BlockSpec((None, tm, d), lambda bi, i: (bi, i + skip, 0))
    prev = pl.BlockSpec((None, SUBLANES, d), lambda bi, i: (bi, jnp.maximum((i + skip) * hb - 1, 0), 0))
    nxt = pl.BlockSpec((None, SUBLANES, d),
                       lambda bi, i: (bi, jnp.minimum((i + skip + 1) * hb, tt // SUBLANES - 1), 0))
    mod = pl.BlockSpec((None, None, 1, d),
                       lambda bi, i: (bi, (i + skip >= n_ctx_blocks).astype(jnp.int32), 0, 0))
    return tile, prev, nxt, mod


def _mod_kernel(cc_ref, w_ref, b_ref, o_ref):
    s = _silu(cc_ref[...])
    o_ref[...] = _mm(s, w_ref[...]) + b_ref[...]


def _modulation(cc, w_mod, b_mod):
    depth, d, _ = w_mod.shape
    bp = cc.shape[0]
    return pl.pallas_call(
        _mod_kernel,
        grid=(depth, 3),
        in_specs=[
            pl.BlockSpec((bp, d), lambda l, n: (0, 0)),
            pl.BlockSpec((None, d, d), lambda l, n: (l, 0, n)),
            pl.BlockSpec((None, None, 1, d), lambda l, n: (l, n, 0, 0)),
        ],
        out_specs=pl.BlockSpec((None, None, bp, d), lambda l, n: (l, n, 0, 0)),
        out_shape=jax.ShapeDtypeStruct((depth, 3, bp, d), F32),
        compiler_params=pltpu.CompilerParams(vmem_limit_bytes=VMEM_LIMIT),
        name="modulation",
    )(cc, w_mod, b_mod.reshape(depth, 3, 1, d))


def _scan_inputs_kernel(x_ref, prev_ref, next_ref, shift_ref, scale_ref, g_ref, w_ref, cos_ref, sin_ref,
                        cw_ref, prm_ref, qkv_ref, gate_ref, *, n_blocks, n_ctx_blocks):
    i = pl.program_id(1)
    tm = x_ref.shape[0]
    h = _modulated_norm(x_ref, prev_ref, next_ref, g_ref, scale_ref, shift_ref)
    ret = jnp.dot(h[:tm], w_ref[:, 0:3 * BR_W], preferred_element_type=F32)
    gdn_ext = jnp.dot(h, w_ref[:, 3 * BR_W:6 * BR_W], preferred_element_type=F32)
    a = jnp.dot(h[:tm], w_ref[:, 6 * BR_W:], preferred_element_type=F32)

    cos = cos_ref[...]
    sin = sin_ref[...]
    first_half = (_iota(cos.shape, 1) & 31) < 16

    def rope(x):
        partner = jnp.where(first_half, -pltpu.roll(x, BR_W - 16, axis=1), pltpu.roll(x, 16, axis=1))
        return x * cos + partner * sin

    qkv_ref[:, 0 * BR_W:1 * BR_W] = rope(ret[:, 0 * BR_W:1 * BR_W]).astype(BF16)
    qkv_ref[:, 1 * BR_W:2 * BR_W] = (rope(ret[:, 1 * BR_W:2 * BR_W]) * HEAD_DIM ** -0.5).astype(BF16)
    qkv_ref[:, 2 * BR_W:3 * BR_W] = ret[:, 2 * BR_W:3 * BR_W].astype(BF16)

    gdn, prev_row, next_row = _halo_rows(gdn_ext, tm, i, n_blocks, n_ctx_blocks)
    g = _silu(_conv3(gdn, prev_row, next_row, cw_ref[...]))
    hsum = _head_mean_matrix() * HEAD_DIM

    def l2n(x):
        return x * lax.rsqrt(_mm(x * x, hsum) + EPS)

    qkv_ref[:, 3 * BR_W:4 * BR_W] = (l2n(g[:, 0:BR_W]) * HEAD_DIM ** -0.5).astype(BF16)
    qkv_ref[:, 4 * BR_W:5 * BR_W] = l2n(g[:, BR_W:2 * BR_W]).astype(BF16)
    qkv_ref[:, 5 * BR_W:6 * BR_W] = g[:, 2 * BR_W:3 * BR_W].astype(BF16)

    neg_rate = -jnp.exp(prm_ref[0:1, :])
    z = a + prm_ref[1:2, :]
    softplus = jnp.maximum(z, 0.0) + jnp.log1p(jnp.exp(-jnp.abs(z)))
    gate_ref[...] = jnp.where(_iota(a.shape, 1) < N_GATES // 2, neg_rate * softplus, jax.nn.sigmoid(a))


def _scan_inputs(xc, shift, scale, g_pre, w_scan, cos_tab, sin_tab, conv_w, prm, *, tm, n_ctx_blocks):
    b, tt, d = xc.shape
    nb = tt // tm
    tile, prev, nxt, mod = _token_tile_specs(tm, d, tt, n_ctx_blocks)
    const2 = lambda bi, i: (0, 0)
    kern = functools.partial(_scan_inputs_kernel, n_blocks=nb, n_ctx_blocks=n_ctx_blocks)
    return pl.pallas_call(
        kern,
        grid=(b, nb),
        in_specs=[
            tile, prev, nxt, mod, mod,
            pl.BlockSpec((1, d), const2),
            pl.BlockSpec((d, SCAN_W), const2),
            pl.BlockSpec((tm, BR_W), lambda bi, i: (i, 0)),
            pl.BlockSpec((tm, BR_W), lambda bi, i: (i, 0)),
            pl.BlockSpec((3, 3 * BR_W), const2),
            pl.BlockSpec((SUBLANES, LANES), const2),
        ],
        out_specs=[
            pl.BlockSpec((None, tm, 6 * BR_W), lambda bi, i: (bi, i, 0)),
            pl.BlockSpec((None, tm, LANES), lambda bi, i: (bi, i, 0)),
        ],
        out_shape=[
            jax.ShapeDtypeStruct((b, tt, 6 * BR_W), BF16),
            jax.ShapeDtypeStruct((b, tt, LANES), F32),
        ],
        compiler_params=pltpu.CompilerParams(vmem_limit_bytes=VMEM_LIMIT),
        name="scan_inputs",
    )(xc, xc, xc, shift, scale, g_pre, w_scan, cos_tab, sin_tab, conv_w, prm)


def _chunk_cumsum(x, chunk, reverse):
    t = x.shape[0]
    pos = _iota(x.shape, 0) & (chunk - 1)
    s = 1
    while s < chunk:
        if reverse:
            x = x + jnp.where(pos < chunk - s, pltpu.roll(x, t - s, axis=0), 0.0)
        else:
            x = x + jnp.where(pos >= s, pltpu.roll(x, s, axis=0), 0.0)
        s *= 2
    return x


def _expand_exact(x, e):
    hi = x.astype(BF16)
    r1 = x - hi.astype(F32)
    mid = r1.astype(BF16)
    lo = (r1 - mid.astype(F32)).astype(BF16)
    eb = e.astype(BF16)
    return (jnp.dot(hi, eb, preferred_element_type=F32) + jnp.dot(mid, eb, preferred_element_type=F32)
            + jnp.dot(lo, eb, preferred_element_type=F32))


def _interleave(chains):
    results = [None] * len(chains)
    live = list(enumerate(chains))
    while live:
        still = []
        for idx, chain in live:
            try:
                next(chain)
                still.append((idx, chain))
            except StopIteration as stop:
                results[idx] = stop.value
        live = still
    return results


def _retention_chain(q, k, v, s_ref, o_ref, intra, qdec, kdec, cdec, bd_mask):
    scores = _mm_nt(q, _blockdiag(k, RET_CHUNK)) * intra
    yield
    s = s_ref[...]
    o_ref[...] += _mm(scores, _blockdiag(v, RET_CHUNK)) + _mm(q, s) * qdec
    yield
    s_ref[...] = s * cdec + jnp.where(bd_mask, _mm_tn(k * kdec, v), 0.0)


def _unit_triangular_inverse(a, row, col, eye):
    c = GDN_CHUNK
    same16 = (row >> 4) == (col >> 4)
    same32 = (row >> 5) == (col >> 5)
    a16 = jnp.where(same16, a, 0.0)
    p = jnp.where(eye, 1.0, 0.0) - a16
    x = _mm(a16, _blockdiag(a16, c))
    yield
    for _ in range(2):
        px = _mm(jnp.concatenate([p, x], axis=0), _blockdiag(x, c))
        yield
        p = p + px[:c]
        x = px[c:]
    p = p + _mm(p, _blockdiag(x, c))
    yield
    for off in (jnp.where(jnp.logical_and(same32, jnp.logical_not(same16)), a, 0.0),
                jnp.where(same32, 0.0, a)):
        y = _mm(p, _blockdiag(off, c))
        yield
        p = p - _mm(y, _blockdiag(p, c))
        yield
    return p


def _gdn_chunk_operands(q, k, v, gc, beta, reverse):
    c = GDN_CHUNK
    row = _iota((c, BR_W), 0)
    col = _iota((c, BR_W), 1) & (c - 1)
    eye = row == col
    incl = (row <= col) if reverse else (row >= col)
    strict = (row < col) if reverse else (row > col)
    gc_row = jnp.sum(jnp.where(eye, gc, 0.0), axis=0, keepdims=True)
    decay = jnp.exp(jnp.where(incl, gc - gc_row, NEG_INF))
    kk_qk = _mm_nt(jnp.concatenate([k, q], axis=0), _blockdiag(k, c))
    yield
    a = jnp.where(strict, kk_qk[:c] * beta * decay, 0.0)
    inv = yield from _unit_triangular_inverse(a, row, col, eye)
    egc = jnp.exp(gc)
    u = _mm(inv, _blockdiag(v * beta, c))
    w = _mm(inv, _blockdiag(k * beta * egc, c))
    yield
    g_last = gc[0:1, :] if reverse else gc[c - 1:c, :]
    return dict(u=u, w=w, qd=q * egc, intra=kk_qk[c:] * decay, k_tail=k * jnp.exp(g_last - gc),
                cdec=jnp.exp(g_last))


def _gdn_recurrence_chain(chunks, s_ref, bd_mask):
    c = GDN_CHUNK
    s = s_ref[...]
    for ch, o_ref in chunks:
        ws_qs = _mm(jnp.concatenate([ch["w"], ch["qd"]], axis=0), s)
        yield
        v_new = ch["u"] - ws_qs[:c]
        o_ref[...] += ws_qs[c:] + _mm(ch["intra"], _blockdiag(v_new, c))
        s = s * ch["cdec"] + jnp.where(bd_mask, _mm_tn(ch["k_tail"], v_new), 0.0)
        yield
    s_ref[...] = s


def _bwd_block(j, n_steps, n_ctx_steps):
    return jnp.where(j < n_ctx_steps, n_ctx_steps - 1 - j, n_steps - 1 - (j - n_ctx_steps))


def _scan_kernel(qkv_f_ref, qkv_b_ref, gate_f_ref, gate_b_ref, intra_ref, qdec_ref, kdec_ref,
                 cdec_ref, o_ref, s_ret_ref, s_gdn_ref, *, n_steps, n_ctx_steps):
    j = pl.program_id(1)

    @pl.when(j == 0)
    def _():
        s_ret_ref[...] = jnp.zeros_like(s_ret_ref)
        s_gdn_ref[...] = jnp.zeros_like(s_gdn_ref)
        o_ref[...] = jnp.zeros_like(o_ref)

    bd_mask = (_iota((BR_W, BR_W), 0) >> 6) == (_iota((BR_W, BR_W), 1) >> 6)
    erow = _iota((LANES, BR_W), 0)
    ehead = _iota((LANES, BR_W), 1) >> 6
    c = GDN_CHUNK
    n_sub = RET_CHUNK // c
    row0 = (pl.multiple_of(j * RET_CHUNK, RET_CHUNK),
            pl.multiple_of(_bwd_block(j, n_steps, n_ctx_steps) * RET_CHUNK, RET_CHUNK))

    chains, recurrences = [], []
    for bi in range(o_ref.shape[0]):
        for d, (qkv_ref, gate_ref) in enumerate(((qkv_f_ref, gate_f_ref), (qkv_b_ref, gate_b_ref))):
            reverse = d == 1
            qkv = qkv_ref[bi].astype(F32)
            chains.append(_retention_chain(
                qkv[:, 0:BR_W], qkv[:, BR_W:2 * BR_W], qkv[:, 2 * BR_W:3 * BR_W], s_ret_ref.at[bi, d],
                o_ref.at[bi, pl.ds(row0[d], RET_CHUNK), 0:BR_W], intra_ref[d], qdec_ref[d], kdec_ref[d],
                cdec_ref[...], bd_mask))
            gates = gate_ref[bi]
            gc_all = _expand_exact(_chunk_cumsum(gates, c, reverse),
                                   jnp.where(erow == N_HEADS * d + ehead, 1.0, 0.0))
            beta_all = _mm(gates, jnp.where(erow == N_GATES // 2 + N_HEADS * d + ehead, 1.0, 0.0))
            chunk_refs = []
            for sub in (range(n_sub - 1, -1, -1) if reverse else range(n_sub)):
                r0 = sub * c
                chains.append(_gdn_chunk_operands(
                    qkv[r0:r0 + c, 3 * BR_W:4 * BR_W], qkv[r0:r0 + c, 4 * BR_W:5 * BR_W],
                    qkv[r0:r0 + c, 5 * BR_W:6 * BR_W], gc_all[r0:r0 + c], beta_all[r0:r0 + c], reverse))
                chunk_refs.append(o_ref.at[bi, pl.ds(row0[d] + r0, c), BR_W:2 * BR_W])
            recurrences.append((len(chains) - n_sub, chunk_refs, s_gdn_ref.at[bi, d]))

    results = _interleave(chains)
    _interleave([_gdn_recurrence_chain(list(zip(results[first:first + n_sub], refs)), s_ref, bd_mask)
                 for first, refs, s_ref in recurrences])


def _scans(qkv, gates, intra, qdec, kdec, cdec, *, n_ctx_steps, group):
    b, tt, _ = qkv.shape
    c = RET_CHUNK
    ns = tt // c
    fwd = lambda bi, j: (bi, j, 0)
    bwd = lambda bi, j: (bi, _bwd_block(j, ns, n_ctx_steps), 0)
    const3 = lambda bi, j: (0, 0, 0)
    kern = functools.partial(_scan_kernel, n_steps=ns, n_ctx_steps=n_ctx_steps)
    return pl.pallas_call(
        kern,
        grid=(b // group, ns),
        in_specs=[
            pl.BlockSpec((group, c, 6 * BR_W), fwd),
            pl.BlockSpec((group, c, 6 * BR_W), bwd),
            pl.BlockSpec((group, c, LANES), fwd),
            pl.BlockSpec((group, c, LANES), bwd),
            pl.BlockSpec((2, c, N_HEADS * c), const3),
            pl.BlockSpec((2, c, BR_W), const3),
            pl.BlockSpec((2, c, BR_W), const3),
            pl.BlockSpec((1, BR_W), lambda bi, j: (0, 0)),
        ],
        out_specs=pl.BlockSpec((group, tt, 2 * BR_W), lambda bi, j: (bi, 0, 0)),
        out_shape=jax.ShapeDtypeStruct((b, tt, 2 * BR_W), F32),
        scratch_shapes=[
            pltpu.VMEM((group, 2, BR_W, BR_W), F32),
            pltpu.VMEM((group, 2, BR_W, BR_W), F32),
        ],
        compiler_params=pltpu.CompilerParams(
            dimension_semantics=("arbitrary", "arbitrary"), vmem_limit_bytes=VMEM_LIMIT),
        name="bidirectional_scans",
    )(qkv, qkv, gates, gates, intra, qdec, kdec, cdec)


def _finish_kernel(o_ref, x_ref, prev_ref, next_ref, shift_ref, scale_ref, gate_ref, gpre_ref, gpost_ref,
                   wfin_ref, wout_ref, sgw_ref, sgb_ref, scw_ref, rng_ref, gng_ref, out_ref, y_ref,
                   *, n_blocks, n_ctx_blocks, skip):
    i = pl.program_id(1) + skip
    tm = x_ref.shape[0]
    h = _modulated_norm(x_ref, prev_ref, next_ref, gpre_ref, scale_ref, shift_ref)
    ch_ext = jnp.dot(h, wfin_ref[:, 0:2 * BR_W], preferred_element_type=F32)
    pf = jnp.dot(h[:tm], wfin_ref[:, 2 * BR_W:], preferred_element_type=F32)
    sc_b, sc_z, ret_z, sg_u, sg_v, sg_z, gdn_z = (pf[:, n * BR_W:(n + 1) * BR_W] for n in range(7))
    hm = _head_mean_matrix()

    o_ret = o_ref[:, 0:BR_W]
    cen = o_ret - _mm(o_ret, hm)
    var = _mm(cen * cen, hm)
    y_ref[:, 0:BR_W] = (cen * lax.rsqrt(var + EPS) * rng_ref[...]) * _silu(ret_z)

    u = _gelu_tanh(sg_u)
    v = _gelu_tanh(sg_v)
    vc = v - jnp.mean(v, axis=-1, keepdims=True)
    v = vc * lax.rsqrt(jnp.mean(vc * vc, axis=-1, keepdims=True) + EPS)
    usz = u * _silu(sg_z)
    sgw = sgw_ref[...]
    for n in range(tm // RET_CHUNK):
        r0 = n * RET_CHUNK
        s = _mm(sgw, _blockdiag(v[r0:r0 + RET_CHUNK], RET_CHUNK)) + sgb_ref[...]
        y_ref[r0:r0 + RET_CHUNK, BR_W:2 * BR_W] = usz[r0:r0 + RET_CHUNK] * s

    ch, prev_row, next_row = _halo_rows(ch_ext[:, 0:BR_W] * ch_ext[:, BR_W:2 * BR_W], tm, i,
                                        n_blocks, n_ctx_blocks)
    y_ref[:, 2 * BR_W:3 * BR_W] = sc_b * _conv3(ch, prev_row, next_row, scw_ref[...]) * _silu(sc_z)

    o_gdn = o_ref[:, BR_W:2 * BR_W]
    ms = _mm(o_gdn * o_gdn, hm)
    y_ref[:, 3 * BR_W:4 * BR_W] = (o_gdn * lax.rsqrt(ms + EPS) * gng_ref[...]) * _silu(gdn_z)

    r = jnp.dot(y_ref[...].astype(BF16), wout_ref[...], preferred_element_type=F32)
    rn = (r * lax.rsqrt(jnp.mean(r * r, axis=-1, keepdims=True) + EPS)) * gpost_ref[...]
    out_ref[...] = x_ref[...] + gate_ref[...] * rn


def _finish(o, xc, shift, scale, gate, g_pre, g_post, w_fin, w_out, sgw, sgb, sc_w, ret_g, gdn_g,
            *, tm, n_ctx_blocks, skip):
    b, tt, d = xc.shape
    nb = tt // tm
    tile, prev, nxt, mod = _token_tile_specs(tm, d, tt, n_ctx_blocks, skip)
    const2 = lambda bi, i: (0, 0)
    kern = functools.partial(_finish_kernel, n_blocks=nb, n_ctx_blocks=n_ctx_blocks, skip=skip)
    return pl.pallas_call(
        kern,
        grid=(b, nb - skip),
        in_specs=[
            pl.BlockSpec((None, tm, 2 * BR_W), lambda bi, i: (bi, i + skip, 0)),
            tile, prev, nxt, mod, mod, mod,
            pl.BlockSpec((1, d), const2),
            pl.BlockSpec((1, d), const2),
            pl.BlockSpec((d, FIN_W), const2),
            pl.BlockSpec((4 * BR_W, d), const2),
            pl.BlockSpec((RET_CHUNK, N_HEADS * RET_CHUNK), const2),
            pl.BlockSpec((RET_CHUNK, BR_W), const2),
            pl.BlockSpec((3, BR_W), const2),
            pl.BlockSpec((1, BR_W), const2),
            pl.BlockSpec((1, BR_W), const2),
        ],
        out_specs=pl.BlockSpec((None, tm, d), lambda bi, i: (bi, i, 0)),
        out_shape=jax.ShapeDtypeStruct((b, tt - skip * tm, d), F32),
        scratch_shapes=[pltpu.VMEM((tm, 4 * BR_W), F32)],
        compiler_params=pltpu.CompilerParams(vmem_limit_bytes=VMEM_LIMIT),
        name="finish",
    )(o, xc, xc, xc, shift, scale, gate, g_pre, g_post, w_fin, w_out, sgw, sgb, sc_w, ret_g, gdn_g)


def _rope_tables(t_lat, t_ctx):
    nf = HEAD_DIM // 4
    inv = ROPE_BASE ** (-jnp.arange(nf, dtype=F32) / nf)
    rows = t_lat // GRID_W
    row = jnp.repeat(jnp.arange(rows), GRID_W).astype(F32)
    col = jnp.tile(jnp.arange(GRID_W), rows).astype(F32)
    ang_r = row[:, None] * inv
    ang_c = col[:, None] * inv
    ang = jnp.concatenate([ang_r, ang_r, ang_c, ang_c], axis=-1)
    cos = jnp.tile(jnp.cos(ang), (1, N_HEADS))
    sin = jnp.tile(jnp.sin(ang), (1, N_HEADS))
    cos = jnp.concatenate([jnp.ones((t_ctx, BR_W), F32), cos], axis=0)
    sin = jnp.concatenate([jnp.zeros((t_ctx, BR_W), F32), sin], axis=0)
    return cos, sin


def _retention_tables():
    c = RET_CHUNK
    log_gamma = jnp.log(1.0 - 2.0 ** (-5.0 - jnp.arange(N_HEADS, dtype=F32)))
    pos = jnp.arange(c, dtype=F32)
    lg = log_gamma[:, None]
    diff = pos[:, None] - pos[None, :]
    intra_f = jnp.exp(jnp.where(diff >= 0, diff * lg[..., None], -jnp.inf))
    intra_b = jnp.swapaxes(intra_f, 1, 2)
    wide = lambda m: jnp.transpose(m, (1, 0, 2)).reshape(c, N_HEADS * c)
    nat = lambda m: jnp.repeat(m.T, HEAD_DIM, axis=1)
    q_f = jnp.exp((pos + 1.0) * lg)
    k_f = jnp.exp((c - 1.0 - pos) * lg)
    intra = jnp.stack([wide(intra_f), wide(intra_b)])
    qdec = jnp.stack([nat(q_f), nat(q_f[:, ::-1])])
    kdec = jnp.stack([nat(k_f), nat(k_f[:, ::-1])])
    cdec = jnp.repeat(jnp.exp(c * log_gamma), HEAD_DIM)[None, :]
    return intra, qdec, kdec, cdec


def _split_w_in(w_in):
    depth, d, _ = w_in.shape
    seg = lambda n: w_in[:, :, n * BR_W:(n + 1) * BR_W]
    pad = jnp.zeros((depth, d, LANES - N_GATES), w_in.dtype)
    w_scan = jnp.concatenate([seg(0), seg(1), seg(2), seg(11), seg(12), seg(13), w_in[:, :, 15 * BR_W:], pad],
                             axis=-1).astype(BF16)
    w_fin = jnp.concatenate([seg(8), seg(9), seg(7), seg(10), seg(3), seg(4), seg(5), seg(6), seg(14)],
                            axis=-1).astype(BF16)
    return w_scan, w_fin


def kernel(x, c, ctx, c_ctx, w_mod, b_mod, g_pre, g_post, w_in, w_out, ret_norm_g, sg_w, sg_b,
           sc_conv_w, gdn_conv_w, gdn_a_log, gdn_dt_bias, gdn_norm_g):
    b, t_lat, d = x.shape
    t_ctx = ctx.shape[1]
    depth = w_mod.shape[0]
    tm = 256
    group = 2 if b % 2 == 0 else 1
    assert t_ctx % tm == 0 and t_lat % tm == 0 and tm % RET_CHUNK == 0 and t_lat % GRID_W == 0
    n_ctx_blocks = t_ctx // tm

    bp = -(-(b + 1) // SUBLANES) * SUBLANES
    cc = jnp.concatenate([c, c_ctx[None, :], jnp.zeros((bp - b - 1, d), F32)], axis=0)
    mod = _modulation(cc, w_mod, b_mod)
    mod = jnp.stack([jnp.broadcast_to(mod[:, :, b:b + 1], (depth, 3, b, d)), mod[:, :, :b]], axis=3)
    mod = mod[:, :, :, :, None, :]

    cos_tab, sin_tab = _rope_tables(t_lat, t_ctx)
    intra, qdec, kdec, cdec = _retention_tables()
    w_scan, w_fin = _split_w_in(w_in)
    w_out_b = w_out.astype(BF16)
    prm = jnp.zeros((depth, SUBLANES, LANES), F32)
    prm = prm.at[:, 0, :N_GATES // 2].set(gdn_a_log.reshape(depth, -1))
    prm = prm.at[:, 1, :N_GATES // 2].set(gdn_dt_bias.reshape(depth, -1))
    sgw = jnp.transpose(sg_w, (0, 2, 1, 3)).reshape(depth, RET_CHUNK, N_HEADS * RET_CHUNK)
    sgb = jnp.repeat(jnp.swapaxes(sg_b, 1, 2), HEAD_DIM, axis=2)
    gdn_g = jnp.tile(gdn_norm_g, (1, N_HEADS))

    xc = jnp.concatenate([ctx, x], axis=1)
    for l in range(depth):
        last = l == depth - 1
        shift, scale, gate = mod[l, 0], mod[l, 1], mod[l, 2]
        qkv, gates = _scan_inputs(xc, shift, scale, g_pre[l][None, :], w_scan[l], cos_tab, sin_tab,
                                  gdn_conv_w[l], prm[l], tm=tm, n_ctx_blocks=n_ctx_blocks)
        o = _scans(qkv, gates, intra, qdec, kdec, cdec, n_ctx_steps=t_ctx // RET_CHUNK, group=group)
        xc = _finish(o, xc, shift, scale, gate, g_pre[l][None, :], g_post[l][None, :], w_fin[l], w_out_b[l],
                     sgw[l], sgb[l], sc_conv_w[l], ret_norm_g[l][None, :], gdn_g[l][None, :],
                     tm=tm, n_ctx_blocks=n_ctx_blocks, skip=n_ctx_blocks if last else 0)
    return xc
```

```python
import functools

import jax
import jax.numpy as jnp
from jax import lax
from jax.experimental import pallas as pl
from jax.experimental.pallas import tpu as pltpu

HEAD_DIM = 64
N_HEADS = 4
BR_W = N_HEADS * HEAD_DIM
GRID_W = 64
RET_CHUNK = 128
GDN_CHUNK = 64
ROPE_BASE = 10000.0
EPS = 1e-6
N_GATES = 16
LANES = 128
SUBLANES = 8
SUB = 256
SCAN_W = 6 * BR_W + LANES
FIN_W = 9 * BR_W
VMEM_LIMIT = 56 * 1024 * 1024

F32 = jnp.float32
BF16 = jnp.bfloat16
NEG_INF = float("-inf")


def _mm(a, b):
    return jnp.dot(a.astype(BF16), b.astype(BF16), preferred_element_type=F32)


def _mm_nt(a, b):
    return lax.dot_general(a.astype(BF16), b.astype(BF16), (((1,), (1,)), ((), ())),
                           preferred_element_type=F32)


def _mm_tn(a, b):
    return lax.dot_general(a.astype(BF16), b.astype(BF16), (((0,), (0,)), ((), ())),
                           preferred_element_type=F32)


def _iota(shape, dim):
    return lax.broadcasted_iota(jnp.int32, shape, dim)


def _silu(x):
    return x * jax.nn.sigmoid(x)


def _gelu_tanh(x):
    return 0.5 * x * (1.0 + jnp.tanh(0.7978845608028654 * (x + 0.044715 * (x * x * x))))


def _head_mean_matrix():
    r = _iota((BR_W, BR_W), 0) >> 6
    c = _iota((BR_W, BR_W), 1) >> 6
    return jnp.where(r == c, 1.0 / HEAD_DIM, 0.0).astype(F32)


def _blockdiag(x, chunk):
    xb = x.astype(BF16)
    zeros = jnp.zeros((chunk, LANES), BF16)
    low_head = _iota((chunk, LANES), 1) < HEAD_DIM
    blocks = []
    for head in range(N_HEADS):
        half = xb[:, (head // 2) * LANES:(head // 2 + 1) * LANES]
        kept = jnp.where(low_head if head % 2 == 0 else jnp.logical_not(low_head), half, zeros)
        blocks.append(jnp.concatenate([kept, zeros] if head < 2 else [zeros, kept], axis=1))
    return jnp.concatenate(blocks, axis=0)


def _interleave(chains):
    results = [None] * len(chains)
    live = list(enumerate(chains))
    while live:
        still = []
        for idx, chain in live:
            try:
                next(chain)
                still.append((idx, chain))
            except StopIteration as stop:
                results[idx] = stop.value
        live = still
    return results


def _token_tile_specs(tm, d, tt, skip=0):
    hb = tm // SUBLANES
    tile = pl.BlockSpec((None, tm, d), lambda bi, i: (bi, i + skip, 0))
    prev = pl.BlockSpec((None, SUBLANES, d), lambda bi, i: (bi, jnp.maximum((i + skip) * hb - 1, 0), 0))
    nxt = pl.BlockSpec((None, SUBLANES, d),
                       lambda bi, i: (bi, jnp.minimum((i + skip + 1) * hb, tt // SUBLANES - 1), 0))
    mod = pl.BlockSpec((None, 2, 1, d), lambda bi, i: (bi, 0, 0, 0))
    return tile, prev, nxt, mod


def _sub_block(s, x_ref, prev_ref, next_ref, shift_ref, scale_ref, g_ref, *, n_blocks, n_ctx_blocks, skip=0):
    n_sub = x_ref.shape[0] // SUB
    r0 = s * SUB
    blk = (pl.program_id(1) + skip) * n_sub + s
    prev8 = prev_ref[...] if s == 0 else x_ref[r0 - SUBLANES:r0, :]
    next8 = next_ref[...] if s == n_sub - 1 else x_ref[r0 + SUB:r0 + SUB + SUBLANES, :]
    x = jnp.concatenate([x_ref[r0:r0 + SUB, :], prev8, next8], axis=0)
    is_ctx = blk < n_ctx_blocks
    scale = jnp.where(is_ctx, scale_ref[0], scale_ref[1])
    shift = jnp.where(is_ctx, shift_ref[0], shift_ref[1])
    y = x * lax.rsqrt(jnp.mean(x * x, axis=-1, keepdims=True) + EPS)
    h = ((y * g_ref[...]) * (1.0 + scale) + shift).astype(BF16)
    has_prev = jnp.logical_and(blk > 0, blk != n_ctx_blocks)
    has_next = jnp.logical_and(blk < n_blocks - 1, blk != n_ctx_blocks - 1)
    return h, has_prev, has_next, is_ctx


def _conv3(ext, has_prev, has_next, w):
    x = ext[:SUB]
    prev_row = jnp.where(has_prev, ext[SUB + SUBLANES - 1:SUB + SUBLANES, :], 0.0)
    next_row = jnp.where(has_next, ext[SUB + SUBLANES:SUB + SUBLANES + 1, :], 0.0)
    row = _iota(x.shape, 0)
    x_prev = jnp.where(row == 0, prev_row, pltpu.roll(x, 1, axis=0))
    x_next = jnp.where(row == SUB - 1, next_row, pltpu.roll(x, SUB - 1, axis=0))
    return x_prev * w[0:1, :] + x * w[1:2, :] + x_next * w[2:3, :]


def _mod_kernel(cc_ref, w_ref, b_ref, o_ref):
    s = _silu(cc_ref[...])
    o_ref[...] = _mm(s, w_ref[...]) + b_ref[...]


def _modulation(cc, w_mod, b_mod):
    depth, d, _ = w_mod.shape
    bp = cc.shape[0]
    return pl.pallas_call(
        _mod_kernel,
        grid=(depth, 3),
        in_specs=[
            pl.BlockSpec((bp, d), lambda l, n: (0, 0)),
            pl.BlockSpec((None, d, d), lambda l, n: (l, 0, n)),
            pl.BlockSpec((None, None, 1, d), lambda l, n: (l, n, 0, 0)),
        ],
        out_specs=pl.BlockSpec((None, None, bp, d), lambda l, n: (l, n, 0, 0)),
        out_shape=jax.ShapeDtypeStruct((depth, 3, bp, d), F32),
        compiler_params=pltpu.CompilerParams(vmem_limit_bytes=VMEM_LIMIT),
        name="modulation",
    )(cc, w_mod, b_mod.reshape(depth, 3, 1, d))


def _scan_inputs_sub_block(s, x_ref, prev_ref, next_ref, shift_ref, scale_ref, g_ref, w_ref, cos_ref,
                           sin_ref, cw_ref, prm_ref, qkv_ref, gate_ref, *, n_blocks, n_ctx_blocks):
    rows = slice(s * SUB, (s + 1) * SUB)
    h, has_prev, has_next, _ = _sub_block(s, x_ref, prev_ref, next_ref, shift_ref, scale_ref, g_ref,
                                          n_blocks=n_blocks, n_ctx_blocks=n_ctx_blocks)
    ret = jnp.dot(h[:SUB], w_ref[:, 0:3 * BR_W], preferred_element_type=F32)
    gdn_ext = jnp.dot(h, w_ref[:, 3 * BR_W:6 * BR_W], preferred_element_type=F32)
    a = jnp.dot(h[:SUB], w_ref[:, 6 * BR_W:], preferred_element_type=F32)
    yield

    cos = cos_ref[rows, :]
    sin = sin_ref[rows, :]
    first_half = (_iota(cos.shape, 1) & 31) < 16

    def rope(x):
        partner = jnp.where(first_half, -pltpu.roll(x, BR_W - 16, axis=1), pltpu.roll(x, 16, axis=1))
        return x * cos + partner * sin

    qkv_ref[rows, 0 * BR_W:1 * BR_W] = rope(ret[:, 0 * BR_W:1 * BR_W]).astype(BF16)
    qkv_ref[rows, 1 * BR_W:2 * BR_W] = (rope(ret[:, 1 * BR_W:2 * BR_W]) * HEAD_DIM ** -0.5).astype(BF16)
    qkv_ref[rows, 2 * BR_W:3 * BR_W] = ret[:, 2 * BR_W:3 * BR_W].astype(BF16)

    g = _silu(_conv3(gdn_ext, has_prev, has_next, cw_ref[...]))
    hsum = _head_mean_matrix() * HEAD_DIM

    def l2n(x):
        return x * lax.rsqrt(_mm(x * x, hsum) + EPS)

    qkv_ref[rows, 3 * BR_W:4 * BR_W] = (l2n(g[:, 0:BR_W]) * HEAD_DIM ** -0.5).astype(BF16)
    qkv_ref[rows, 4 * BR_W:5 * BR_W] = l2n(g[:, BR_W:2 * BR_W]).astype(BF16)
    qkv_ref[rows, 5 * BR_W:6 * BR_W] = g[:, 2 * BR_W:3 * BR_W].astype(BF16)

    neg_rate = -jnp.exp(prm_ref[0:1, :])
    z = a + prm_ref[1:2, :]
    softplus = jnp.maximum(z, 0.0) + jnp.log1p(jnp.exp(-jnp.abs(z)))
    gate_ref[rows, :] = jnp.where(_iota(a.shape, 1) < N_GATES // 2, neg_rate * softplus, jax.nn.sigmoid(a))


def _scan_inputs_kernel(*refs, n_blocks, n_ctx_blocks):
    n_sub = refs[0].shape[0] // SUB
    _interleave([_scan_inputs_sub_block(s, *refs, n_blocks=n_blocks, n_ctx_blocks=n_ctx_blocks)
                 for s in range(n_sub)])


def _scan_inputs(xc, shift, scale, g_pre, w_scan, cos_tab, sin_tab, conv_w, prm, *, tm, n_ctx_blocks):
    b, tt, d = xc.shape
    tile, prev, nxt, mod = _token_tile_specs(tm, d, tt)
    const2 = lambda bi, i: (0, 0)
    kern = functools.partial(_scan_inputs_kernel, n_blocks=tt // SUB, n_ctx_blocks=n_ctx_blocks)
    return pl.pallas_call(
        kern,
        grid=(b, tt // tm),
        in_specs=[
            tile, prev, nxt, mod, mod,
            pl.BlockSpec((1, d), const2),
            pl.BlockSpec((d, SCAN_W), const2),
            pl.BlockSpec((tm, BR_W), lambda bi, i: (i, 0)),
            pl.BlockSpec((tm, BR_W), lambda bi, i: (i, 0)),
            pl.BlockSpec((3, 3 * BR_W), const2),
            pl.BlockSpec((SUBLANES, LANES), const2),
        ],
        out_specs=[
            pl.BlockSpec((None, tm, 6 * BR_W), lambda bi, i: (bi, i, 0)),
            pl.BlockSpec((None, tm, LANES), lambda bi, i: (bi, i, 0)),
        ],
        out_shape=[
            jax.ShapeDtypeStruct((b, tt, 6 * BR_W), BF16),
            jax.ShapeDtypeStruct((b, tt, LANES), F32),
        ],
        compiler_params=pltpu.CompilerParams(vmem_limit_bytes=VMEM_LIMIT),
        name="scan_inputs",
    )(xc, xc, xc, shift, scale, g_pre, w_scan, cos_tab, sin_tab, conv_w, prm)


def _chunk_cumsum(x, chunk, reverse):
    t = x.shape[0]
    pos = _iota(x.shape, 0) & (chunk - 1)
    s = 1
    while s < chunk:
        if reverse:
            x = x + jnp.where(pos < chunk - s, pltpu.roll(x, t - s, axis=0), 0.0)
        else:
            x = x + jnp.where(pos >= s, pltpu.roll(x, s, axis=0), 0.0)
        s *= 2
    return x


def _expand_exact(x, e):
    hi = x.astype(BF16)
    r1 = x - hi.astype(F32)
    mid = r1.astype(BF16)
    lo = (r1 - mid.astype(F32)).astype(BF16)
    eb = e.astype(BF16)
    return (jnp.dot(hi, eb, preferred_element_type=F32) + jnp.dot(mid, eb, preferred_element_type=F32)
            + jnp.dot(lo, eb, preferred_element_type=F32))


def _retention_chain(q, k, v, s_ref, o_ref, intra, qdec, kdec, cdec, bd_mask):
    scores = _mm_nt(q, _blockdiag(k, RET_CHUNK)) * intra
    yield
    s = s_ref[...]
    o_ref[...] += _mm(scores, _blockdiag(v, RET_CHUNK)) + _mm(q, s) * qdec
    yield
    s_ref[...] = s * cdec + jnp.where(bd_mask, _mm_tn(k * kdec, v), 0.0)


def _unit_triangular_inverse(a, row, col, eye):
    c = GDN_CHUNK
    same16 = (row >> 4) == (col >> 4)
    same32 = (row >> 5) == (col >> 5)
    a16 = jnp.where(same16, a, 0.0)
    p = jnp.where(eye, 1.0, 0.0) - a16
    x = _mm(a16, _blockdiag(a16, c))
    yield
    for _ in range(2):
        px = _mm(jnp.concatenate([p, x], axis=0), _blockdiag(x, c))
        yield
        p = p + px[:c]
        x = px[c:]
    p = p + _mm(p, _blockdiag(x, c))
    yield
    for off in (jnp.where(jnp.logical_and(same32, jnp.logical_not(same16)), a, 0.0),
                jnp.where(same32, 0.0, a)):
        y = _mm(p, _blockdiag(off, c))
        yield
        p = p - _mm(y, _blockdiag(p, c))
        yield
    return p


def _gdn_chunk_operands(q, k, v, gc, beta, reverse):
    c = GDN_CHUNK
    row = _iota((c, BR_W), 0)
    col = _iota((c, BR_W), 1) & (c - 1)
    eye = row == col
    incl = (row <= col) if reverse else (row >= col)
    strict = (row < col) if reverse else (row > col)
    gc_row = jnp.sum(jnp.where(eye, gc, 0.0), axis=0, keepdims=True)
    decay = jnp.exp(jnp.where(incl, gc - gc_row, NEG_INF))
    kk_qk = _mm_nt(jnp.concatenate([k, q], axis=0), _blockdiag(k, c))
    yield
    a = jnp.where(strict, kk_qk[:c] * beta * decay, 0.0)
    inv = yield from _unit_triangular_inverse(a, row, col, eye)
    egc = jnp.exp(gc)
    u = _mm(inv, _blockdiag(v * beta, c))
    w = _mm(inv, _blockdiag(k * beta * egc, c))
    yield
    g_last = gc[0:1, :] if reverse else gc[c - 1:c, :]
    return dict(u=u, w=w, qd=q * egc, intra=kk_qk[c:] * decay, k_tail=k * jnp.exp(g_last - gc),
                cdec=jnp.exp(g_last))


def _gdn_recurrence_chain(chunks, s_ref, bd_mask):
    c = GDN_CHUNK
    s = s_ref[...]
    for ch, o_ref in chunks:
        ws_qs = _mm(jnp.concatenate([ch["w"], ch["qd"]], axis=0), s)
        yield
        v_new = ch["u"] - ws_qs[:c]
        o_ref[...] += ws_qs[c:] + _mm(ch["intra"], _blockdiag(v_new, c))
        s = s * ch["cdec"] + jnp.where(bd_mask, _mm_tn(ch["k_tail"], v_new), 0.0)
        yield
    s_ref[...] = s


def _bwd_block(j, n_steps, n_ctx_steps):
    return jnp.where(j < n_ctx_steps, n_ctx_steps - 1 - j, n_steps - 1 - (j - n_ctx_steps))


def _scan_kernel(qkv_f_ref, qkv_b_ref, gate_f_ref, gate_b_ref, intra_ref, qdec_ref, kdec_ref,
                 cdec_ref, o_ref, s_ret_ref, s_gdn_ref, *, n_steps, n_ctx_steps):
    j = pl.program_id(1)

    @pl.when(j == 0)
    def _():
        s_ret_ref[...] = jnp.zeros_like(s_ret_ref)
        s_gdn_ref[...] = jnp.zeros_like(s_gdn_ref)
        o_ref[...] = jnp.zeros_like(o_ref)

    bd_mask = (_iota((BR_W, BR_W), 0) >> 6) == (_iota((BR_W, BR_W), 1) >> 6)
    erow = _iota((LANES, BR_W), 0)
    ehead = _iota((LANES, BR_W), 1) >> 6
    c = GDN_CHUNK
    n_sub = RET_CHUNK // c
    row0 = (pl.multiple_of(j * RET_CHUNK, RET_CHUNK),
            pl.multiple_of(_bwd_block(j, n_steps, n_ctx_steps) * RET_CHUNK, RET_CHUNK))

    chains, recurrences = [], []
    for bi in range(o_ref.shape[0]):
        for d, (qkv_ref, gate_ref) in enumerate(((qkv_f_ref, gate_f_ref), (qkv_b_ref, gate_b_ref))):
            reverse = d == 1
            qkv = qkv_ref[bi].astype(F32)
            chains.append(_retention_chain(
                qkv[:, 0:BR_W], qkv[:, BR_W:2 * BR_W], qkv[:, 2 * BR_W:3 * BR_W], s_ret_ref.at[bi, d],
                o_ref.at[bi, pl.ds(row0[d], RET_CHUNK), 0:BR_W], intra_ref[d], qdec_ref[d], kdec_ref[d],
                cdec_ref[...], bd_mask))
            gates = gate_ref[bi]
            gc_all = _expand_exact(_chunk_cumsum(gates, c, reverse),
                                   jnp.where(erow == N_HEADS * d + ehead, 1.0, 0.0))
            beta_all = _mm(gates, jnp.where(erow == N_GATES // 2 + N_HEADS * d + ehead, 1.0, 0.0))
            chunk_refs = []
            for sub in (range(n_sub - 1, -1, -1) if reverse else range(n_sub)):
                r0 = sub * c
                chains.append(_gdn_chunk_operands(
                    qkv[r0:r0 + c, 3 * BR_W:4 * BR_W], qkv[r0:r0 + c, 4 * BR_W:5 * BR_W],
                    qkv[r0:r0 + c, 5 * BR_W:6 * BR_W], gc_all[r0:r0 + c], beta_all[r0:r0 + c], reverse))
                chunk_refs.append(o_ref.at[bi, pl.ds(row0[d] + r0, c), BR_W:2 * BR_W])
            recurrences.append((len(chains) - n_sub, chunk_refs, s_gdn_ref.at[bi, d]))

    results = _interleave(chains)
    _interleave([_gdn_recurrence_chain(list(zip(results[first:first + n_sub], refs)), s_ref, bd_mask)
                 for first, refs, s_ref in recurrences])


def _scans(qkv, gates, intra, qdec, kdec, cdec, *, n_ctx_steps, group):
    b, tt, _ = qkv.shape
    c = RET_CHUNK
    ns = tt // c
    fwd = lambda bi, j: (bi, j, 0)
    bwd = lambda bi, j: (bi, _bwd_block(j, ns, n_ctx_steps), 0)
    const3 = lambda bi, j: (0, 0, 0)
    kern = functools.partial(_scan_kernel, n_steps=ns, n_ctx_steps=n_ctx_steps)
    return pl.pallas_call(
        kern,
        grid=(b // group, ns),
        in_specs=[
            pl.BlockSpec((group, c, 6 * BR_W), fwd),
            pl.BlockSpec((group, c, 6 * BR_W), bwd),
            pl.BlockSpec((group, c, LANES), fwd),
            pl.BlockSpec((group, c, LANES), bwd),
            pl.BlockSpec((2, c, N_HEADS * c), const3),
            pl.BlockSpec((2, c, BR_W), const3),
            pl.BlockSpec((2, c, BR_W), const3),
            pl.BlockSpec((1, BR_W), lambda bi, j: (0, 0)),
        ],
        out_specs=pl.BlockSpec((group, tt, 2 * BR_W), lambda bi, j: (bi, 0, 0)),
        out_shape=jax.ShapeDtypeStruct((b, tt, 2 * BR_W), F32),
        scratch_shapes=[
            pltpu.VMEM((group, 2, BR_W, BR_W), F32),
            pltpu.VMEM((group, 2, BR_W, BR_W), F32),
        ],
        compiler_params=pltpu.CompilerParams(
            dimension_semantics=("arbitrary", "arbitrary"), vmem_limit_bytes=VMEM_LIMIT),
        name="bidirectional_scans",
    )(qkv, qkv, gates, gates, intra, qdec, kdec, cdec)


def _finish_sub_block(s, o_ref, x_ref, prev_ref, next_ref, shift_ref, scale_ref, gate_ref, gpre_ref,
                      gpost_ref, wfin_ref, wout_ref, sgw_ref, sgb_ref, scw_ref, rng_ref, gng_ref, out_ref,
                      y_ref, *, n_blocks, n_ctx_blocks, skip):
    rows = slice(s * SUB, (s + 1) * SUB)
    h, has_prev, has_next, is_ctx = _sub_block(s, x_ref, prev_ref, next_ref, shift_ref, scale_ref, gpre_ref,
                                               n_blocks=n_blocks, n_ctx_blocks=n_ctx_blocks, skip=skip)
    ch_ext = jnp.dot(h, wfin_ref[:, 0:2 * BR_W], preferred_element_type=F32)
    pf = jnp.dot(h[:SUB], wfin_ref[:, 2 * BR_W:], preferred_element_type=F32)
    yield
    sc_b, sc_z, ret_z, sg_u, sg_v, sg_z, gdn_z = (pf[:, n * BR_W:(n + 1) * BR_W] for n in range(7))
    hm = _head_mean_matrix()

    o_ret = o_ref[rows, 0:BR_W]
    cen = o_ret - _mm(o_ret, hm)
    var = _mm(cen * cen, hm)
    y_ref[rows, 0:BR_W] = (cen * lax.rsqrt(var + EPS) * rng_ref[...]) * _silu(ret_z)

    u = _gelu_tanh(sg_u)
    v = _gelu_tanh(sg_v)
    vc = v - jnp.mean(v, axis=-1, keepdims=True)
    v = vc * lax.rsqrt(jnp.mean(vc * vc, axis=-1, keepdims=True) + EPS)
    usz = u * _silu(sg_z)
    sgw = sgw_ref[...]
    for r0 in range(0, SUB, RET_CHUNK):
        sg = _mm(sgw, _blockdiag(v[r0:r0 + RET_CHUNK], RET_CHUNK)) + sgb_ref[...]
        y_ref[rows.start + r0:rows.start + r0 + RET_CHUNK, BR_W:2 * BR_W] = usz[r0:r0 + RET_CHUNK] * sg

    conv = _conv3(ch_ext[:, 0:BR_W] * ch_ext[:, BR_W:2 * BR_W], has_prev, has_next, scw_ref[...])
    y_ref[rows, 2 * BR_W:3 * BR_W] = sc_b * conv * _silu(sc_z)

    o_gdn = o_ref[rows, BR_W:2 * BR_W]
    ms = _mm(o_gdn * o_gdn, hm)
    y_ref[rows, 3 * BR_W:4 * BR_W] = (o_gdn * lax.rsqrt(ms + EPS) * gng_ref[...]) * _silu(gdn_z)
    yield

    r = jnp.dot(y_ref[rows, :].astype(BF16), wout_ref[...], preferred_element_type=F32)
    yield
    gate = jnp.where(is_ctx, gate_ref[0], gate_ref[1])
    rn = (r * lax.rsqrt(jnp.mean(r * r, axis=-1, keepdims=True) + EPS)) * gpost_ref[...]
    out_ref[rows, :] = x_ref[rows, :] + gate * rn


def _finish_kernel(*refs, n_blocks, n_ctx_blocks, skip):
    n_sub = refs[1].shape[0] // SUB
    _interleave([_finish_sub_block(s, *refs, n_blocks=n_blocks, n_ctx_blocks=n_ctx_blocks, skip=skip)
                 for s in range(n_sub)])


def _finish(o, xc, shift, scale, gate, g_pre, g_post, w_fin, w_out, sgw, sgb, sc_w, ret_g, gdn_g,
            *, tm, n_ctx_blocks, skip):
    b, tt, d = xc.shape
    tile, prev, nxt, mod = _token_tile_specs(tm, d, tt, skip)
    const2 = lambda bi, i: (0, 0)
    kern = functools.partial(_finish_kernel, n_blocks=tt // SUB, n_ctx_blocks=n_ctx_blocks, skip=skip)
    return pl.pallas_call(
        kern,
        grid=(b, tt // tm - skip),
        in_specs=[
            pl.BlockSpec((None, tm, 2 * BR_W), lambda bi, i: (bi, i + skip, 0)),
            tile, prev, nxt, mod, mod, mod,
            pl.BlockSpec((1, d), const2),
            pl.BlockSpec((1, d), const2),
            pl.BlockSpec((d, FIN_W), const2),
            pl.BlockSpec((4 * BR_W, d), const2),
            pl.BlockSpec((RET_CHUNK, N_HEADS * RET_CHUNK), const2),
            pl.BlockSpec((RET_CHUNK, BR_W), const2),
            pl.BlockSpec((3, BR_W), const2),
            pl.BlockSpec((1, BR_W), const2),
            pl.BlockSpec((1, BR_W), const2),
        ],
        out_specs=pl.BlockSpec((None, tm, d), lambda bi, i: (bi, i, 0)),
        out_shape=jax.ShapeDtypeStruct((b, tt - skip * tm, d), F32),
        scratch_shapes=[pltpu.VMEM((tm, 4 * BR_W), F32)],
        compiler_params=pltpu.CompilerParams(vmem_limit_bytes=VMEM_LIMIT),
        name="finish",
    )(o, xc, xc, xc, shift, scale, gate, g_pre, g_post, w_fin, w_out, sgw, sgb, sc_w, ret_g, gdn_g)


def _rope_tables(t_lat, t_ctx):
    nf = HEAD_DIM // 4
    inv = ROPE_BASE ** (-jnp.arange(nf, dtype=F32) / nf)
    rows = t_lat // GRID_W
    row = jnp.repeat(jnp.arange(rows), GRID_W).astype(F32)
    col = jnp.tile(jnp.arange(GRID_W), rows).astype(F32)
    ang_r = row[:, None] * inv
    ang_c = col[:, None] * inv
    ang = jnp.concatenate([ang_r, ang_r, ang_c, ang_c], axis=-1)
    cos = jnp.tile(jnp.cos(ang), (1, N_HEADS))
    sin = jnp.tile(jnp.sin(ang), (1, N_HEADS))
    cos = jnp.concatenate([jnp.ones((t_ctx, BR_W), F32), cos], axis=0)
    sin = jnp.concatenate([jnp.zeros((t_ctx, BR_W), F32), sin], axis=0)
    return cos, sin


def _retention_tables():
    c = RET_CHUNK
    log_gamma = jnp.log(1.0 - 2.0 ** (-5.0 - jnp.arange(N_HEADS, dtype=F32)))
    pos = jnp.arange(c, dtype=F32)
    lg = log_gamma[:, None]
    diff = pos[:, None] - pos[None, :]
    intra_f = jnp.exp(jnp.where(diff >= 0, diff * lg[..., None], -jnp.inf))
    intra_b = jnp.swapaxes(intra_f, 1, 2)
    wide = lambda m: jnp.transpose(m, (1, 0, 2)).reshape(c, N_HEADS * c)
    nat = lambda m: jnp.repeat(m.T, HEAD_DIM, axis=1)
    q_f = jnp.exp((pos + 1.0) * lg)
    k_f = jnp.exp((c - 1.0 - pos) * lg)
    intra = jnp.stack([wide(intra_f), wide(intra_b)])
    qdec = jnp.stack([nat(q_f), nat(q_f[:, ::-1])])
    kdec = jnp.stack([nat(k_f), nat(k_f[:, ::-1])])
    cdec = jnp.repeat(jnp.exp(c * log_gamma), HEAD_DIM)[None, :]
    return intra, qdec, kdec, cdec


def _split_w_in(w_in):
    depth, d, _ = w_in.shape
    seg = lambda n: w_in[:, :, n * BR_W:(n + 1) * BR_W]
    pad = jnp.zeros((depth, d, LANES - N_GATES), w_in.dtype)
    w_scan = jnp.concatenate([seg(0), seg(1), seg(2), seg(11), seg(12), seg(13), w_in[:, :, 15 * BR_W:], pad],
                             axis=-1).astype(BF16)
    w_fin = jnp.concatenate([seg(8), seg(9), seg(7), seg(10), seg(3), seg(4), seg(5), seg(6), seg(14)],
                            axis=-1).astype(BF16)
    return w_scan, w_fin


def _token_tile(tt):
    n_blocks = tt // SUB
    return SUB * max(n for n in (1, 2, 3) if n_blocks % n == 0)


def kernel(x, c, ctx, c_ctx, w_mod, b_mod, g_pre, g_post, w_in, w_out, ret_norm_g, sg_w, sg_b,
           sc_conv_w, gdn_conv_w, gdn_a_log, gdn_dt_bias, gdn_norm_g):
    b, t_lat, d = x.shape
    t_ctx = ctx.shape[1]
    depth = w_mod.shape[0]
    assert t_ctx % SUB == 0 and t_lat % SUB == 0 and SUB % RET_CHUNK == 0 and t_lat % GRID_W == 0
    tm = _token_tile(t_ctx + t_lat)
    n_ctx_blocks = t_ctx // SUB
    group = 2 if b % 2 == 0 else 1

    bp = -(-(b + 1) // SUBLANES) * SUBLANES
    cc = jnp.concatenate([c, c_ctx[None, :], jnp.zeros((bp - b - 1, d), F32)], axis=0)
    mod = _modulation(cc, w_mod, b_mod)
    mod = jnp.stack([jnp.broadcast_to(mod[:, :, b:b + 1], (depth, 3, b, d)), mod[:, :, :b]], axis=3)
    mod = mod[:, :, :, :, None, :]

    cos_tab, sin_tab = _rope_tables(t_lat, t_ctx)
    intra, qdec, kdec, cdec = _retention_tables()
    w_scan, w_fin = _split_w_in(w_in)
    w_out_b = w_out.astype(BF16)
    prm = jnp.zeros((depth, SUBLANES, LANES), F32)
    prm = prm.at[:, 0, :N_GATES // 2].set(gdn_a_log.reshape(depth, -1))
    prm = prm.at[:, 1, :N_GATES // 2].set(gdn_dt_bias.reshape(depth, -1))
    sgw = jnp.transpose(sg_w, (0, 2, 1, 3)).reshape(depth, RET_CHUNK, N_HEADS * RET_CHUNK)
    sgb = jnp.repeat(jnp.swapaxes(sg_b, 1, 2), HEAD_DIM, axis=2)
    gdn_g = jnp.tile(gdn_norm_g, (1, N_HEADS))

    xc = jnp.concatenate([ctx, x], axis=1)
    for l in range(depth):
        last = l == depth - 1
        shift, scale, gate = mod[l, 0], mod[l, 1], mod[l, 2]
        qkv, gates = _scan_inputs(xc, shift, scale, g_pre[l][None, :], w_scan[l], cos_tab, sin_tab,
                                  gdn_conv_w[l], prm[l], tm=tm, n_ctx_blocks=n_ctx_blocks)
        o = _scans(qkv, gates, intra, qdec, kdec, cdec, n_ctx_steps=t_ctx // RET_CHUNK, group=group)
        xc = _finish(o, xc, shift, scale, gate, g_pre[l][None, :], g_post[l][None, :], w_fin[l], w_out_b[l],
                     sgw[l], sgb[l], sc_conv_w[l], ret_norm_g[l][None, :], gdn_g[l][None, :],
                     tm=SUB if last else tm, n_ctx_blocks=n_ctx_blocks, skip=n_ctx_blocks if last else 0)
    return xc
```

```python
import functools

import jax
import jax.numpy as jnp
import numpy as np
from jax import lax
from jax.experimental import pallas as pl
from jax.experimental.pallas import tpu as pltpu

HEAD_DIM = 64
N_HEADS = 4
BR_W = N_HEADS * HEAD_DIM
GRID_W = 64
RET_CHUNK = 128
GDN_CHUNK = 64
ROPE_BASE = 10000.0
EPS = 1e-6
N_GATES = 16
LANES = 128
SUBLANES = 8
SUB = 256
SCAN_W = 6 * BR_W + LANES
FIN_W = 9 * BR_W
VMEM_LIMIT = 56 * 1024 * 1024

F32 = jnp.float32
BF16 = jnp.bfloat16
NEG_INF = float("-inf")


def _mm(a, b):
    return jnp.dot(a.astype(BF16), b.astype(BF16), preferred_element_type=F32)


def _mm_nt(a, b):
    return lax.dot_general(a.astype(BF16), b.astype(BF16), (((1,), (1,)), ((), ())),
                           preferred_element_type=F32)


def _mm_tn(a, b):
    return lax.dot_general(a.astype(BF16), b.astype(BF16), (((0,), (0,)), ((), ())),
                           preferred_element_type=F32)


def _iota(shape, dim):
    return lax.broadcasted_iota(jnp.int32, shape, dim)


def _silu(x):
    return x * jax.nn.sigmoid(x)


def _gelu_tanh(x):
    return 0.5 * x * (1.0 + jnp.tanh(0.7978845608028654 * (x + 0.044715 * (x * x * x))))


def _head_mean_matrix():
    r = _iota((BR_W, BR_W), 0) >> 6
    c = _iota((BR_W, BR_W), 1) >> 6
    return jnp.where(r == c, 1.0 / HEAD_DIM, 0.0).astype(F32)


def _blockdiag(x, chunk):
    xb = x.astype(BF16)
    zeros = jnp.zeros((chunk, LANES), BF16)
    low_head = _iota((chunk, LANES), 1) < HEAD_DIM
    blocks = []
    for head in range(N_HEADS):
        half = xb[:, (head // 2) * LANES:(head // 2 + 1) * LANES]
        kept = jnp.where(low_head if head % 2 == 0 else jnp.logical_not(low_head), half, zeros)
        blocks.append(jnp.concatenate([kept, zeros] if head < 2 else [zeros, kept], axis=1))
    return jnp.concatenate(blocks, axis=0)


def _interleave(chains):
    results = [None] * len(chains)
    live = list(enumerate(chains))
    while live:
        still = []
        for idx, chain in live:
            try:
                next(chain)
                still.append((idx, chain))
            except StopIteration as stop:
                results[idx] = stop.value
        live = still
    return results


def _token_tile_specs(tm, d, tt, skip=0):
    hb = tm // SUBLANES
    tile = pl.BlockSpec((None, tm, d), lambda bi, i: (bi, i + skip, 0))
    prev = pl.BlockSpec((None, SUBLANES, d), lambda bi, i: (bi, jnp.maximum((i + skip) * hb - 1, 0), 0))
    nxt = pl.BlockSpec((None, SUBLANES, d),
                       lambda bi, i: (bi, jnp.minimum((i + skip + 1) * hb, tt // SUBLANES - 1), 0))
    mod = pl.BlockSpec((None, 2, 1, d), lambda bi, i: (bi, 0, 0, 0))
    return tile, prev, nxt, mod


def _sub_block(s, x_ref, prev_ref, next_ref, shift_ref, scale_ref, g_ref, *, n_blocks, n_ctx_blocks, skip=0):
    n_sub = x_ref.shape[0] // SUB
    r0 = s * SUB
    blk = (pl.program_id(1) + skip) * n_sub + s
    prev8 = prev_ref[...] if s == 0 else x_ref[r0 - SUBLANES:r0, :]
    next8 = next_ref[...] if s == n_sub - 1 else x_ref[r0 + SUB:r0 + SUB + SUBLANES, :]
    x = jnp.concatenate([x_ref[r0:r0 + SUB, :], prev8, next8], axis=0)
    is_ctx = blk < n_ctx_blocks
    scale = jnp.where(is_ctx, scale_ref[0], scale_ref[1])
    shift = jnp.where(is_ctx, shift_ref[0], shift_ref[1])
    gain = g_ref[...] * (1.0 + scale)
    h = (x * lax.rsqrt(jnp.mean(x * x, axis=-1, keepdims=True) + EPS) * gain + shift).astype(BF16)
    has_prev = jnp.logical_and(blk > 0, blk != n_ctx_blocks)
    has_next = jnp.logical_and(blk < n_blocks - 1, blk != n_ctx_blocks - 1)
    return h, has_prev, has_next, is_ctx


def _conv3(ext, has_prev, has_next, w):
    x = ext[:SUB]
    prev_row = jnp.where(has_prev, ext[SUB + SUBLANES - 1:SUB + SUBLANES, :], 0.0)
    next_row = jnp.where(has_next, ext[SUB + SUBLANES:SUB + SUBLANES + 1, :], 0.0)
    row = _iota(x.shape, 0)
    x_prev = jnp.where(row == 0, prev_row, pltpu.roll(x, 1, axis=0))
    x_next = jnp.where(row == SUB - 1, next_row, pltpu.roll(x, SUB - 1, axis=0))
    return x_prev * w[0:1, :] + x * w[1:2, :] + x_next * w[2:3, :]


def _mod_kernel(cc_ref, w_ref, b_ref, o_ref):
    s = _silu(cc_ref[...])
    o_ref[...] = _mm(s, w_ref[...]) + b_ref[...]


def _modulation(cc, w_mod, b_mod):
    depth, d, _ = w_mod.shape
    bp = cc.shape[0]
    return pl.pallas_call(
        _mod_kernel,
        grid=(depth, 3),
        in_specs=[
            pl.BlockSpec((bp, d), lambda l, n: (0, 0)),
            pl.BlockSpec((None, d, d), lambda l, n: (l, 0, n)),
            pl.BlockSpec((None, None, 1, d), lambda l, n: (l, n, 0, 0)),
        ],
        out_specs=pl.BlockSpec((None, None, bp, d), lambda l, n: (l, n, 0, 0)),
        out_shape=jax.ShapeDtypeStruct((depth, 3, bp, d), F32),
        compiler_params=pltpu.CompilerParams(vmem_limit_bytes=VMEM_LIMIT),
        name="modulation",
    )(cc, w_mod, b_mod.reshape(depth, 3, 1, d))


def _scan_inputs_sub_block(s, x_ref, prev_ref, next_ref, shift_ref, scale_ref, g_ref, w_ref, cos_ref,
                           sin_ref, cw_ref, prm_ref, qkv_ref, gate_ref, *, n_blocks, n_ctx_blocks):
    rows = slice(s * SUB, (s + 1) * SUB)
    h, has_prev, has_next, _ = _sub_block(s, x_ref, prev_ref, next_ref, shift_ref, scale_ref, g_ref,
                                          n_blocks=n_blocks, n_ctx_blocks=n_ctx_blocks)
    ret = jnp.dot(h[:SUB], w_ref[:, 0:3 * BR_W], preferred_element_type=F32)
    gdn_ext = jnp.dot(h, w_ref[:, 3 * BR_W:6 * BR_W], preferred_element_type=F32)
    a = jnp.dot(h[:SUB], w_ref[:, 6 * BR_W:], preferred_element_type=F32)
    yield

    cos = cos_ref[rows, :]
    sin = sin_ref[rows, :]
    first_half = (_iota(cos.shape, 1) & 31) < 16

    def rope(x):
        partner = jnp.where(first_half, pltpu.roll(x, BR_W - 16, axis=1), pltpu.roll(x, 16, axis=1))
        return x * cos + partner * sin

    qkv_ref[rows, 0 * BR_W:1 * BR_W] = rope(ret[:, 0 * BR_W:1 * BR_W]).astype(BF16)
    qkv_ref[rows, 1 * BR_W:2 * BR_W] = (rope(ret[:, 1 * BR_W:2 * BR_W]) * HEAD_DIM ** -0.5).astype(BF16)
    qkv_ref[rows, 2 * BR_W:3 * BR_W] = ret[:, 2 * BR_W:3 * BR_W].astype(BF16)

    g = _silu(_conv3(gdn_ext, has_prev, has_next, cw_ref[...]))
    hsum = _head_mean_matrix() * HEAD_DIM

    def l2n(x):
        return x * lax.rsqrt(_mm(x * x, hsum) + EPS)

    qkv_ref[rows, 3 * BR_W:4 * BR_W] = (l2n(g[:, 0:BR_W]) * HEAD_DIM ** -0.5).astype(BF16)
    qkv_ref[rows, 4 * BR_W:5 * BR_W] = l2n(g[:, BR_W:2 * BR_W]).astype(BF16)
    qkv_ref[rows, 5 * BR_W:6 * BR_W] = g[:, 2 * BR_W:3 * BR_W].astype(BF16)

    neg_rate = -jnp.exp(prm_ref[0:1, :])
    z = a + prm_ref[1:2, :]
    softplus = jnp.maximum(z, 0.0) + jnp.log1p(jnp.exp(-jnp.abs(z)))
    gate_ref[rows, :] = jnp.where(_iota(a.shape, 1) < N_GATES // 2, neg_rate * softplus, jax.nn.sigmoid(a))


def _scan_inputs_kernel(*refs, n_blocks, n_ctx_blocks):
    n_sub = refs[0].shape[0] // SUB
    _interleave([_scan_inputs_sub_block(s, *refs, n_blocks=n_blocks, n_ctx_blocks=n_ctx_blocks)
                 for s in range(n_sub)])


def _scan_inputs(xc, shift, scale, g_pre, w_scan, cos_tab, sin_tab, conv_w, prm, *, tm, n_ctx_blocks):
    b, tt, d = xc.shape
    tile, prev, nxt, mod = _token_tile_specs(tm, d, tt)
    const2 = lambda bi, i: (0, 0)
    kern = functools.partial(_scan_inputs_kernel, n_blocks=tt // SUB, n_ctx_blocks=n_ctx_blocks)
    return pl.pallas_call(
        kern,
        grid=(b, tt // tm),
        in_specs=[
            tile, prev, nxt, mod, mod,
            pl.BlockSpec((1, d), const2),
            pl.BlockSpec((d, SCAN_W), const2),
            pl.BlockSpec((tm, BR_W), lambda bi, i: (i, 0)),
            pl.BlockSpec((tm, BR_W), lambda bi, i: (i, 0)),
            pl.BlockSpec((3, 3 * BR_W), const2),
            pl.BlockSpec((SUBLANES, LANES), const2),
        ],
        out_specs=[
            pl.BlockSpec((None, tm, 6 * BR_W), lambda bi, i: (bi, i, 0)),
            pl.BlockSpec((None, tm, LANES), lambda bi, i: (bi, i, 0)),
        ],
        out_shape=[
            jax.ShapeDtypeStruct((b, tt, 6 * BR_W), BF16),
            jax.ShapeDtypeStruct((b, tt, LANES), F32),
        ],
        compiler_params=pltpu.CompilerParams(vmem_limit_bytes=VMEM_LIMIT),
        name="scan_inputs",
    )(xc, xc, xc, shift, scale, g_pre, w_scan, cos_tab, sin_tab, conv_w, prm)


def _chunk_cumsum(x, chunk, reverse):
    t = x.shape[0]
    pos = _iota(x.shape, 0) & (chunk - 1)
    s = 1
    while s < chunk:
        if reverse:
            x = x + jnp.where(pos < chunk - s, pltpu.roll(x, t - s, axis=0), 0.0)
        else:
            x = x + jnp.where(pos >= s, pltpu.roll(x, s, axis=0), 0.0)
        s *= 2
    return x


def _expand_heads(x, first):
    t = x.shape[0]
    head = _iota((t, BR_W), 1) >> 6
    out = jnp.broadcast_to(x[:, first:first + 1], (t, BR_W))
    for h in range(1, N_HEADS):
        out = jnp.where(head == h, jnp.broadcast_to(x[:, first + h:first + h + 1], (t, BR_W)), out)
    return out


def _retention_chain(q, k, v, s_ref, o_ref, intra, qdec, kdec, cdec, bd_mask):
    scores = _mm_nt(q, _blockdiag(k, RET_CHUNK)) * intra
    yield
    s = s_ref[...]
    o_ref[...] += _mm(scores, _blockdiag(v, RET_CHUNK)) + _mm(q, s) * qdec
    yield
    s_ref[...] = s * cdec + jnp.where(bd_mask, _mm_tn(k * kdec, v), 0.0)


def _unit_triangular_inverse(a, row, col, eye):
    c = GDN_CHUNK
    same16 = (row >> 4) == (col >> 4)
    same32 = (row >> 5) == (col >> 5)
    a16 = jnp.where(same16, a, 0.0)
    p = jnp.where(eye, 1.0, 0.0) - a16
    x = _mm(a16, _blockdiag(a16, c))
    yield
    for _ in range(2):
        px = _mm(jnp.concatenate([p, x], axis=0), _blockdiag(x, c))
        yield
        p = p + px[:c]
        x = px[c:]
    p = p + _mm(p, _blockdiag(x, c))
    yield
    for off in (jnp.where(jnp.logical_and(same32, jnp.logical_not(same16)), a, 0.0),
                jnp.where(same32, 0.0, a)):
        y = _mm(p, _blockdiag(off, c))
        yield
        p = p - _mm(y, _blockdiag(p, c))
        yield
    return p


def _gdn_chunk_operands(q, k, v, gc, beta, reverse):
    c = GDN_CHUNK
    row = _iota((c, BR_W), 0)
    col = _iota((c, BR_W), 1) & (c - 1)
    eye = row == col
    incl = (row <= col) if reverse else (row >= col)
    strict = (row < col) if reverse else (row > col)
    gc_row = jnp.sum(jnp.where(eye, gc, 0.0), axis=0, keepdims=True)
    decay = jnp.exp(jnp.where(incl, gc - gc_row, NEG_INF))
    kk_qk = _mm_nt(jnp.concatenate([k, q], axis=0), _blockdiag(k, c))
    yield
    a = jnp.where(strict, kk_qk[:c] * beta * decay, 0.0)
    inv = yield from _unit_triangular_inverse(a, row, col, eye)
    egc = jnp.exp(gc)
    u = _mm(inv, _blockdiag(v * beta, c))
    w = _mm(inv, _blockdiag(k * beta * egc, c))
    yield
    g_last = gc[0:1, :] if reverse else gc[c - 1:c, :]
    return dict(u=u, w=w, qd=q * egc, intra=kk_qk[c:] * decay, k_tail=k * jnp.exp(g_last - gc),
                cdec=jnp.exp(g_last))


def _gdn_recurrence_chain(chunks, s_ref, bd_mask):
    c = GDN_CHUNK
    s = s_ref[...]
    for ch, o_ref in chunks:
        ws_qs = _mm(jnp.concatenate([ch["w"], ch["qd"]], axis=0), s)
        yield
        v_new = ch["u"] - ws_qs[:c]
        o_ref[...] += ws_qs[c:] + _mm(ch["intra"], _blockdiag(v_new, c))
        s = s * ch["cdec"] + jnp.where(bd_mask, _mm_tn(ch["k_tail"], v_new), 0.0)
        yield
    s_ref[...] = s


def _bwd_block(j, n_steps, n_ctx_steps):
    return jnp.where(j < n_ctx_steps, n_ctx_steps - 1 - j, n_steps - 1 - (j - n_ctx_steps))


def _scan_kernel(qkv_f_ref, qkv_b_ref, gate_f_ref, gate_b_ref, intra_ref, qdec_ref, kdec_ref,
                 cdec_ref, o_ref, s_ret_ref, s_gdn_ref, *, n_steps, n_ctx_steps):
    j = pl.program_id(1)

    @pl.when(j == 0)
    def _():
        s_ret_ref[...] = jnp.zeros_like(s_ret_ref)
        s_gdn_ref[...] = jnp.zeros_like(s_gdn_ref)
        o_ref[...] = jnp.zeros_like(o_ref)

    bd_mask = (_iota((BR_W, BR_W), 0) >> 6) == (_iota((BR_W, BR_W), 1) >> 6)
    c = GDN_CHUNK
    n_sub = RET_CHUNK // c
    row0 = (pl.multiple_of(j * RET_CHUNK, RET_CHUNK),
            pl.multiple_of(_bwd_block(j, n_steps, n_ctx_steps) * RET_CHUNK, RET_CHUNK))

    chains, recurrences = [], []
    for bi in range(o_ref.shape[0]):
        for d, (qkv_ref, gate_ref) in enumerate(((qkv_f_ref, gate_f_ref), (qkv_b_ref, gate_b_ref))):
            reverse = d == 1
            qkv = qkv_ref[bi].astype(F32)
            chains.append(_retention_chain(
                qkv[:, 0:BR_W], qkv[:, BR_W:2 * BR_W], qkv[:, 2 * BR_W:3 * BR_W], s_ret_ref.at[bi, d],
                o_ref.at[bi, pl.ds(row0[d], RET_CHUNK), 0:BR_W], intra_ref[d], qdec_ref[d], kdec_ref[d],
                cdec_ref[...], bd_mask))
            gates = gate_ref[bi]
            gc_all = _expand_heads(_chunk_cumsum(gates, c, reverse), N_HEADS * d)
            beta_all = _expand_heads(gates, N_GATES // 2 + N_HEADS * d)
            chunk_refs = []
            for sub in (range(n_sub - 1, -1, -1) if reverse else range(n_sub)):
                r0 = sub * c
                chains.append(_gdn_chunk_operands(
                    qkv[r0:r0 + c, 3 * BR_W:4 * BR_W], qkv[r0:r0 + c, 4 * BR_W:5 * BR_W],
                    qkv[r0:r0 + c, 5 * BR_W:6 * BR_W], gc_all[r0:r0 + c], beta_all[r0:r0 + c], reverse))
                chunk_refs.append(o_ref.at[bi, pl.ds(row0[d] + r0, c), BR_W:2 * BR_W])
            recurrences.append((len(chains) - n_sub, chunk_refs, s_gdn_ref.at[bi, d]))

    results = _interleave(chains)
    _interleave([_gdn_recurrence_chain(list(zip(results[first:first + n_sub], refs)), s_ref, bd_mask)
                 for first, refs, s_ref in recurrences])


def _scans(qkv, gates, intra, qdec, kdec, cdec, *, n_ctx_steps, group):
    b, tt, _ = qkv.shape
    c = RET_CHUNK
    ns = tt // c
    fwd = lambda bi, j: (bi, j, 0)
    bwd = lambda bi, j: (bi, _bwd_block(j, ns, n_ctx_steps), 0)
    const3 = lambda bi, j: (0, 0, 0)
    kern = functools.partial(_scan_kernel, n_steps=ns, n_ctx_steps=n_ctx_steps)
    return pl.pallas_call(
        kern,
        grid=(b // group, ns),
        in_specs=[
            pl.BlockSpec((group, c, 6 * BR_W), fwd),
            pl.BlockSpec((group, c, 6 * BR_W), bwd),
            pl.BlockSpec((group, c, LANES), fwd),
            pl.BlockSpec((group, c, LANES), bwd),
            pl.BlockSpec((2, c, N_HEADS * c), const3),
            pl.BlockSpec((2, c, BR_W), const3),
            pl.BlockSpec((2, c, BR_W), const3),
            pl.BlockSpec((1, BR_W), lambda bi, j: (0, 0)),
        ],
        out_specs=pl.BlockSpec((group, tt, 2 * BR_W), lambda bi, j: (bi, 0, 0)),
        out_shape=jax.ShapeDtypeStruct((b, tt, 2 * BR_W), F32),
        scratch_shapes=[
            pltpu.VMEM((group, 2, BR_W, BR_W), F32),
            pltpu.VMEM((group, 2, BR_W, BR_W), F32),
        ],
        compiler_params=pltpu.CompilerParams(
            dimension_semantics=("arbitrary", "arbitrary"), vmem_limit_bytes=VMEM_LIMIT),
        name="bidirectional_scans",
    )(qkv, qkv, gates, gates, intra, qdec, kdec, cdec)


def _finish_sub_block(s, o_ref, x_ref, prev_ref, next_ref, shift_ref, scale_ref, gate_ref, gpre_ref,
                      gpost_ref, wfin_ref, wout_ref, sgw_ref, sgb_ref, scw_ref, rng_ref, gng_ref, out_ref,
                      y_ref, *, n_blocks, n_ctx_blocks, skip):
    rows = slice(s * SUB, (s + 1) * SUB)
    h, has_prev, has_next, is_ctx = _sub_block(s, x_ref, prev_ref, next_ref, shift_ref, scale_ref, gpre_ref,
                                               n_blocks=n_blocks, n_ctx_blocks=n_ctx_blocks, skip=skip)
    ch_ext = jnp.dot(h, wfin_ref[:, 0:2 * BR_W], preferred_element_type=F32)
    pf = jnp.dot(h[:SUB], wfin_ref[:, 2 * BR_W:], preferred_element_type=F32)
    yield
    sc_b, sc_z, ret_z, sg_u, sg_v, sg_z, gdn_z = (pf[:, n * BR_W:(n + 1) * BR_W] for n in range(7))
    hm = _head_mean_matrix()

    o_ret = o_ref[rows, 0:BR_W]
    cen = o_ret - _mm(o_ret, hm)
    var = _mm(cen * cen, hm)
    y_ref[rows, 0:BR_W] = (cen * lax.rsqrt(var + EPS) * rng_ref[...]) * _silu(ret_z)

    u = _gelu_tanh(sg_u)
    v = _gelu_tanh(sg_v)
    vc = v - jnp.mean(v, axis=-1, keepdims=True)
    v = vc * lax.rsqrt(jnp.mean(vc * vc, axis=-1, keepdims=True) + EPS)
    usz = u * _silu(sg_z)
    sgw = sgw_ref[...]
    for r0 in range(0, SUB, RET_CHUNK):
        sg = _mm(sgw, _blockdiag(v[r0:r0 + RET_CHUNK], RET_CHUNK)) + sgb_ref[...]
        y_ref[rows.start + r0:rows.start + r0 + RET_CHUNK, BR_W:2 * BR_W] = usz[r0:r0 + RET_CHUNK] * sg

    conv = _conv3(ch_ext[:, 0:BR_W] * ch_ext[:, BR_W:2 * BR_W], has_prev, has_next, scw_ref[...])
    y_ref[rows, 2 * BR_W:3 * BR_W] = sc_b * conv * _silu(sc_z)

    o_gdn = o_ref[rows, BR_W:2 * BR_W]
    ms = _mm(o_gdn * o_gdn, hm)
    y_ref[rows, 3 * BR_W:4 * BR_W] = (o_gdn * lax.rsqrt(ms + EPS) * gng_ref[...]) * _silu(gdn_z)
    yield

    r = jnp.dot(y_ref[rows, :].astype(BF16), wout_ref[...], preferred_element_type=F32)
    yield
    gate = jnp.where(is_ctx, gate_ref[0], gate_ref[1])
    rn = (r * lax.rsqrt(jnp.mean(r * r, axis=-1, keepdims=True) + EPS)) * gpost_ref[...]
    out_ref[rows, :] = x_ref[rows, :] + gate * rn


def _finish_kernel(*refs, n_blocks, n_ctx_blocks, skip):
    n_sub = refs[1].shape[0] // SUB
    _interleave([_finish_sub_block(s, *refs, n_blocks=n_blocks, n_ctx_blocks=n_ctx_blocks, skip=skip)
                 for s in range(n_sub)])


def _finish(o, xc, shift, scale, gate, g_pre, g_post, w_fin, w_out, sgw, sgb, sc_w, ret_g, gdn_g,
            *, tm, n_ctx_blocks, skip):
    b, tt, d = xc.shape
    tile, prev, nxt, mod = _token_tile_specs(tm, d, tt, skip)
    const2 = lambda bi, i: (0, 0)
    kern = functools.partial(_finish_kernel, n_blocks=tt // SUB, n_ctx_blocks=n_ctx_blocks, skip=skip)
    return pl.pallas_call(
        kern,
        grid=(b, tt // tm - skip),
        in_specs=[
            pl.BlockSpec((None, tm, 2 * BR_W), lambda bi, i: (bi, i + skip, 0)),
            tile, prev, nxt, mod, mod, mod,
            pl.BlockSpec((1, d), const2),
            pl.BlockSpec((1, d), const2),
            pl.BlockSpec((d, FIN_W), const2),
            pl.BlockSpec((4 * BR_W, d), const2),
            pl.BlockSpec((RET_CHUNK, N_HEADS * RET_CHUNK), const2),
            pl.BlockSpec((RET_CHUNK, BR_W), const2),
            pl.BlockSpec((3, BR_W), const2),
            pl.BlockSpec((1, BR_W), const2),
            pl.BlockSpec((1, BR_W), const2),
        ],
        out_specs=pl.BlockSpec((None, tm, d), lambda bi, i: (bi, i, 0)),
        out_shape=jax.ShapeDtypeStruct((b, tt - skip * tm, d), F32),
        scratch_shapes=[pltpu.VMEM((tm, 4 * BR_W), F32)],
        compiler_params=pltpu.CompilerParams(vmem_limit_bytes=VMEM_LIMIT),
        name="finish",
    )(o, xc, xc, xc, shift, scale, gate, g_pre, g_post, w_fin, w_out, sgw, sgb, sc_w, ret_g, gdn_g)


def _rope_tables(t_lat, t_ctx):
    nf = HEAD_DIM // 4
    inv = np.float32(ROPE_BASE) ** (-np.arange(nf, dtype=np.float32) / np.float32(nf))
    rows = t_lat // GRID_W
    row = np.repeat(np.arange(rows), GRID_W).astype(np.float32)
    col = np.tile(np.arange(GRID_W), rows).astype(np.float32)
    ang_r = row[:, None] * inv
    ang_c = col[:, None] * inv
    ang = np.concatenate([ang_r, ang_r, ang_c, ang_c], axis=-1)
    cos = np.tile(np.cos(ang), (1, N_HEADS))
    first_half = (np.arange(HEAD_DIM) % (HEAD_DIM // 2)) < nf
    sin = np.tile(np.where(first_half, -np.sin(ang), np.sin(ang)), (1, N_HEADS))
    cos = np.concatenate([np.ones((t_ctx, BR_W), np.float32), cos], axis=0)
    sin = np.concatenate([np.zeros((t_ctx, BR_W), np.float32), sin], axis=0)
    return jnp.asarray(cos, F32), jnp.asarray(sin, F32)


def _retention_tables():
    c = RET_CHUNK
    log_gamma = np.log(1.0 - 2.0 ** (-5.0 - np.arange(N_HEADS, dtype=np.float32))).astype(np.float32)
    pos = np.arange(c, dtype=np.float32)
    lg = log_gamma[:, None]
    diff = pos[:, None] - pos[None, :]
    intra_f = np.exp(np.where(diff >= 0, diff * lg[..., None], -np.inf))
    intra_b = np.swapaxes(intra_f, 1, 2)
    wide = lambda m: np.transpose(m, (1, 0, 2)).reshape(c, N_HEADS * c)
    nat = lambda m: np.repeat(m.T, HEAD_DIM, axis=1)
    q_f = np.exp((pos + 1.0) * lg)
    k_f = np.exp((c - 1.0 - pos) * lg)
    intra = np.stack([wide(intra_f), wide(intra_b)])
    qdec = np.stack([nat(q_f), nat(q_f[:, ::-1])])
    kdec = np.stack([nat(k_f), nat(k_f[:, ::-1])])
    cdec = np.repeat(np.exp(c * log_gamma), HEAD_DIM)[None, :]
    return tuple(jnp.asarray(t, F32) for t in (intra, qdec, kdec, cdec))


def _split_w_in(w_in):
    depth, d, _ = w_in.shape
    wb = w_in.astype(BF16)
    seg = lambda n: wb[:, :, n * BR_W:(n + 1) * BR_W]
    pad = jnp.zeros((depth, d, LANES - N_GATES), BF16)
    w_scan = jnp.concatenate([seg(0), seg(1), seg(2), seg(11), seg(12), seg(13), wb[:, :, 15 * BR_W:], pad], axis=-1)
    w_fin = jnp.concatenate([seg(8), seg(9), seg(7), seg(10), seg(3), seg(4), seg(5), seg(6), seg(14)], axis=-1)
    return w_scan, w_fin


def _token_tile(tt):
    n_blocks = tt // SUB
    return SUB * max(n for n in (1, 2, 3) if n_blocks % n == 0)


def kernel(x, c, ctx, c_ctx, w_mod, b_mod, g_pre, g_post, w_in, w_out, ret_norm_g, sg_w, sg_b,
           sc_conv_w, gdn_conv_w, gdn_a_log, gdn_dt_bias, gdn_norm_g):
    b, t_lat, d = x.shape
    t_ctx = ctx.shape[1]
    depth = w_mod.shape[0]
    assert t_ctx % SUB == 0 and t_lat % SUB == 0 and SUB % RET_CHUNK == 0 and t_lat % GRID_W == 0
    tm = _token_tile(t_ctx + t_lat)
    n_ctx_blocks = t_ctx // SUB
    group = 2 if b % 2 == 0 else 1

    bp = -(-(b + 1) // SUBLANES) * SUBLANES
    cc = jnp.concatenate([c, c_ctx[None, :], jnp.zeros((bp - b - 1, d), F32)], axis=0)
    mod = _modulation(cc, w_mod, b_mod)
    mod = jnp.stack([jnp.broadcast_to(mod[:, :, b:b + 1], (depth, 3, b, d)), mod[:, :, :b]], axis=3)
    mod = mod[:, :, :, :, None, :]

    cos_tab, sin_tab = _rope_tables(t_lat, t_ctx)
    intra, qdec, kdec, cdec = _retention_tables()
    w_scan, w_fin = _split_w_in(w_in)
    w_out_b = w_out.astype(BF16)
    prm = jnp.zeros((depth, SUBLANES, LANES), F32)
    prm = prm.at[:, 0, :N_GATES // 2].set(gdn_a_log.reshape(depth, -1))
    prm = prm.at[:, 1, :N_GATES // 2].set(gdn_dt_bias.reshape(depth, -1))
    sgw = jnp.transpose(sg_w, (0, 2, 1, 3)).reshape(depth, RET_CHUNK, N_HEADS * RET_CHUNK)
    sgb = jnp.repeat(jnp.swapaxes(sg_b, 1, 2), HEAD_DIM, axis=2)
    gdn_g = jnp.tile(gdn_norm_g, (1, N_HEADS))

    xc = jnp.concatenate([ctx, x], axis=1)
    for l in range(depth):
        last = l == depth - 1
        shift, scale, gate = mod[l, 0], mod[l, 1], mod[l, 2]
        qkv, gates = _scan_inputs(xc, shift, scale, g_pre[l][None, :], w_scan[l], cos_tab, sin_tab,
                                  gdn_conv_w[l], prm[l], tm=tm, n_ctx_blocks=n_ctx_blocks)
        o = _scans(qkv, gates, intra, qdec, kdec, cdec, n_ctx_steps=t_ctx // RET_CHUNK, group=group)
        xc = _finish(o, xc, shift, scale, gate, g_pre[l][None, :], g_post[l][None, :], w_fin[l], w_out_b[l],
                     sgw[l], sgb[l], sc_conv_w[l], ret_norm_g[l][None, :], gdn_g[l][None, :],
                     tm=SUB if last else tm, n_ctx_blocks=n_ctx_blocks, skip=n_ctx_blocks if last else 0)
    return xc
```

```python
import functools

import jax
import jax.numpy as jnp
import numpy as np
from jax import lax
from jax.experimental import pallas as pl
from jax.experimental.pallas import tpu as pltpu

HEAD_DIM = 64
N_HEADS = 4
BR_W = N_HEADS * HEAD_DIM
GRID_W = 64
RET_CHUNK = 128
GDN_CHUNK = 64
ROPE_BASE = 10000.0
EPS = 1e-6
N_GATES = 16
LANES = 128
SUBLANES = 8
SUB = 256
SCAN_W = 6 * BR_W + LANES
FIN_W = 9 * BR_W
VMEM_LIMIT = 56 * 1024 * 1024

F32 = jnp.float32
BF16 = jnp.bfloat16
NEG_INF = float("-inf")


def _mm(a, b):
    return jnp.dot(a.astype(BF16), b.astype(BF16), preferred_element_type=F32)


def _mm_nt(a, b):
    return lax.dot_general(a.astype(BF16), b.astype(BF16), (((1,), (1,)), ((), ())),
                           preferred_element_type=F32)


def _mm_tn(a, b):
    return lax.dot_general(a.astype(BF16), b.astype(BF16), (((0,), (0,)), ((), ())),
                           preferred_element_type=F32)


def _iota(shape, dim):
    return lax.broadcasted_iota(jnp.int32, shape, dim)


def _silu(x):
    return x * jax.nn.sigmoid(x)


def _gelu_tanh(x):
    return 0.5 * x * (1.0 + jnp.tanh(0.7978845608028654 * (x + 0.044715 * (x * x * x))))


def _head_mean_matrix():
    r = _iota((BR_W, BR_W), 0) >> 6
    c = _iota((BR_W, BR_W), 1) >> 6
    return jnp.where(r == c, 1.0 / HEAD_DIM, 0.0).astype(F32)


def _blockdiag(x, chunk):
    xb = x.astype(BF16)
    zeros = jnp.zeros((chunk, LANES), BF16)
    low_head = _iota((chunk, LANES), 1) < HEAD_DIM
    blocks = []
    for head in range(N_HEADS):
        half = xb[:, (head // 2) * LANES:(head // 2 + 1) * LANES]
        kept = jnp.where(low_head if head % 2 == 0 else jnp.logical_not(low_head), half, zeros)
        blocks.append(jnp.concatenate([kept, zeros] if head < 2 else [zeros, kept], axis=1))
    return jnp.concatenate(blocks, axis=0)


def _interleave(chains):
    results = [None] * len(chains)
    live = list(enumerate(chains))
    while live:
        still = []
        for idx, chain in live:
            try:
                next(chain)
                still.append((idx, chain))
            except StopIteration as stop:
                results[idx] = stop.value
        live = still
    return results


def _token_tile_specs(tm, d, tt, skip=0):
    hb = tm // SUBLANES
    tile = pl.BlockSpec((None, tm, d), lambda bi, i: (bi, i + skip, 0))
    prev = pl.BlockSpec((None, SUBLANES, d), lambda bi, i: (bi, jnp.maximum((i + skip) * hb - 1, 0), 0))
    nxt = pl.BlockSpec((None, SUBLANES, d),
                       lambda bi, i: (bi, jnp.minimum((i + skip + 1) * hb, tt // SUBLANES - 1), 0))
    mod = pl.BlockSpec((None, 2, 1, d), lambda bi, i: (bi, 0, 0, 0))
    return [tile, prev, nxt], mod


def _split_tile_specs(tm, d, t_ctx, t_lat):
    n_sub, n_ctx, hb = tm // SUB, t_ctx // SUB, SUB // SUBLANES

    def rows8(first, t):
        return pl.BlockSpec((None, SUBLANES, d),
                            lambda bi, i: (bi, jnp.clip(first(i), 0, t // SUBLANES - 1), 0))

    ctx_blocks = [pl.BlockSpec((None, SUB, d), lambda bi, i, s=s: (bi, jnp.minimum(i * n_sub + s, n_ctx - 1), 0))
                  for s in range(n_sub)]
    lat_blocks = [pl.BlockSpec((None, SUB, d), lambda bi, i, s=s: (bi, jnp.maximum(i * n_sub + s - n_ctx, 0), 0))
                  for s in range(n_sub)]
    halos = [rows8(lambda i: i * n_sub * hb - 1, t_ctx), rows8(lambda i: (i * n_sub - n_ctx) * hb - 1, t_lat),
             rows8(lambda i: (i + 1) * n_sub * hb, t_ctx), rows8(lambda i: ((i + 1) * n_sub - n_ctx) * hb, t_lat)]
    return ctx_blocks + lat_blocks + halos


def _stream_sources(stream, tm, skip=0):
    if isinstance(stream, tuple):
        ctx, lat = stream
        (b, t_ctx, d), t_lat = ctx.shape, lat.shape[1]
        n_sub = tm // SUB
        specs = _split_tile_specs(tm, d, t_ctx, t_lat)
        operands = [ctx] * n_sub + [lat] * n_sub + [ctx, lat, ctx, lat]
        mod = pl.BlockSpec((None, 2, 1, d), lambda bi, i: (bi, 0, 0, 0))
        return specs, operands, mod, b, t_ctx + t_lat, d
    b, tt, d = stream.shape
    specs, mod = _token_tile_specs(tm, d, tt, skip)
    return specs, [stream] * 3, mod, b, tt, d


def _tile_sources(refs, n_sub, n_ctx_blocks, split):
    if not split:
        x_ref, prev_ref, next_ref = refs[:3]
        return ([x_ref[s * SUB:(s + 1) * SUB, :] for s in range(n_sub)], prev_ref[...], next_ref[...],
                refs[3:])
    ctx_refs, lat_refs = refs[:n_sub], refs[n_sub:2 * n_sub]
    ctx_prev, lat_prev, ctx_next, lat_next = refs[2 * n_sub:2 * n_sub + 4]
    blk0 = pl.program_id(1) * n_sub
    xs = [jnp.where(blk0 + s < n_ctx_blocks, ctx_refs[s][...], lat_refs[s][...]) for s in range(n_sub)]
    prev8 = jnp.where(blk0 - 1 < n_ctx_blocks, ctx_prev[...], lat_prev[...])
    next8 = jnp.where(blk0 + n_sub < n_ctx_blocks, ctx_next[...], lat_next[...])
    return xs, prev8, next8, refs[2 * n_sub + 4:]


def _sub_block(s, xs, prev8, next8, shift_ref, scale_ref, g_ref, *, n_blocks, n_ctx_blocks, skip=0):
    n_sub = len(xs)
    blk = (pl.program_id(1) + skip) * n_sub + s
    before = prev8 if s == 0 else xs[s - 1][SUB - SUBLANES:, :]
    after = next8 if s == n_sub - 1 else xs[s + 1][:SUBLANES, :]
    x = jnp.concatenate([xs[s], before, after], axis=0)
    is_ctx = blk < n_ctx_blocks
    scale = jnp.where(is_ctx, scale_ref[0], scale_ref[1])
    shift = jnp.where(is_ctx, shift_ref[0], shift_ref[1])
    gain = g_ref[...] * (1.0 + scale)
    h = (x * lax.rsqrt(jnp.mean(x * x, axis=-1, keepdims=True) + EPS) * gain + shift).astype(BF16)
    has_prev = jnp.logical_and(blk > 0, blk != n_ctx_blocks)
    has_next = jnp.logical_and(blk < n_blocks - 1, blk != n_ctx_blocks - 1)
    return h, has_prev, has_next, is_ctx


def _conv3(ext, has_prev, has_next, w):
    x = ext[:SUB]
    prev_row = jnp.where(has_prev, ext[SUB + SUBLANES - 1:SUB + SUBLANES, :], 0.0)
    next_row = jnp.where(has_next, ext[SUB + SUBLANES:SUB + SUBLANES + 1, :], 0.0)
    row = _iota(x.shape, 0)
    x_prev = jnp.where(row == 0, prev_row, pltpu.roll(x, 1, axis=0))
    x_next = jnp.where(row == SUB - 1, next_row, pltpu.roll(x, SUB - 1, axis=0))
    return x_prev * w[0:1, :] + x * w[1:2, :] + x_next * w[2:3, :]


def _mod_kernel(cc_ref, w_ref, b_ref, o_ref):
    s = _silu(cc_ref[...])
    o_ref[...] = _mm(s, w_ref[...]) + b_ref[...]


def _modulation(cc, w_mod, b_mod):
    depth, d, _ = w_mod.shape
    bp = cc.shape[0]
    return pl.pallas_call(
        _mod_kernel,
        grid=(depth, 3),
        in_specs=[
            pl.BlockSpec((bp, d), lambda l, n: (0, 0)),
            pl.BlockSpec((None, d, d), lambda l, n: (l, 0, n)),
            pl.BlockSpec((None, None, 1, d), lambda l, n: (l, n, 0, 0)),
        ],
        out_specs=pl.BlockSpec((None, None, bp, d), lambda l, n: (l, n, 0, 0)),
        out_shape=jax.ShapeDtypeStruct((depth, 3, bp, d), F32),
        compiler_params=pltpu.CompilerParams(vmem_limit_bytes=VMEM_LIMIT),
        name="modulation",
    )(cc, w_mod, b_mod.reshape(depth, 3, 1, d))


def _scan_inputs_sub_block(s, xs, prev8, next8, shift_ref, scale_ref, g_ref, w_ref, cos_ref,
                           sin_ref, cw_ref, prm_ref, qkv_ref, gate_ref, *, n_blocks, n_ctx_blocks):
    rows = slice(s * SUB, (s + 1) * SUB)
    h, has_prev, has_next, _ = _sub_block(s, xs, prev8, next8, shift_ref, scale_ref, g_ref,
                                          n_blocks=n_blocks, n_ctx_blocks=n_ctx_blocks)
    ret = jnp.dot(h[:SUB], w_ref[:, 0:3 * BR_W], preferred_element_type=F32)
    gdn_ext = jnp.dot(h, w_ref[:, 3 * BR_W:6 * BR_W], preferred_element_type=F32)
    a = jnp.dot(h[:SUB], w_ref[:, 6 * BR_W:], preferred_element_type=F32)
    yield

    cos = cos_ref[rows, :]
    sin = sin_ref[rows, :]
    first_half = (_iota(cos.shape, 1) & 31) < 16

    def rope(x):
        partner = jnp.where(first_half, pltpu.roll(x, BR_W - 16, axis=1), pltpu.roll(x, 16, axis=1))
        return x * cos + partner * sin

    qkv_ref[rows, 0 * BR_W:1 * BR_W] = rope(ret[:, 0 * BR_W:1 * BR_W]).astype(BF16)
    qkv_ref[rows, 1 * BR_W:2 * BR_W] = (rope(ret[:, 1 * BR_W:2 * BR_W]) * HEAD_DIM ** -0.5).astype(BF16)
    qkv_ref[rows, 2 * BR_W:3 * BR_W] = ret[:, 2 * BR_W:3 * BR_W].astype(BF16)

    g = _silu(_conv3(gdn_ext, has_prev, has_next, cw_ref[...]))
    hsum = _head_mean_matrix() * HEAD_DIM

    def l2n(x):
        return x * lax.rsqrt(_mm(x * x, hsum) + EPS)

    qkv_ref[rows, 3 * BR_W:4 * BR_W] = (l2n(g[:, 0:BR_W]) * HEAD_DIM ** -0.5).astype(BF16)
    qkv_ref[rows, 4 * BR_W:5 * BR_W] = l2n(g[:, BR_W:2 * BR_W]).astype(BF16)
    qkv_ref[rows, 5 * BR_W:6 * BR_W] = g[:, 2 * BR_W:3 * BR_W].astype(BF16)

    neg_rate = -jnp.exp(prm_ref[0:1, :])
    z = a + prm_ref[1:2, :]
    softplus = jnp.maximum(z, 0.0) + jnp.log1p(jnp.exp(-jnp.abs(z)))
    gate_ref[rows, :] = jnp.where(_iota(a.shape, 1) < N_GATES // 2, neg_rate * softplus, jax.nn.sigmoid(a))


def _scan_inputs_kernel(*refs, n_sub, n_blocks, n_ctx_blocks, split):
    xs, prev8, next8, rest = _tile_sources(refs, n_sub, n_ctx_blocks, split)
    _interleave([_scan_inputs_sub_block(s, xs, prev8, next8, *rest, n_blocks=n_blocks, n_ctx_blocks=n_ctx_blocks)
                 for s in range(n_sub)])


def _scan_inputs(stream, shift, scale, g_pre, w_scan, cos_tab, sin_tab, conv_w, prm, *, tm, n_ctx_blocks):
    specs, operands, mod, b, tt, d = _stream_sources(stream, tm)
    const2 = lambda bi, i: (0, 0)
    kern = functools.partial(_scan_inputs_kernel, n_sub=tm // SUB, n_blocks=tt // SUB,
                             n_ctx_blocks=n_ctx_blocks, split=isinstance(stream, tuple))
    return pl.pallas_call(
        kern,
        grid=(b, tt // tm),
        in_specs=[
            *specs, mod, mod,
            pl.BlockSpec((1, d), const2),
            pl.BlockSpec((d, SCAN_W), const2),
            pl.BlockSpec((tm, BR_W), lambda bi, i: (i, 0)),
            pl.BlockSpec((tm, BR_W), lambda bi, i: (i, 0)),
            pl.BlockSpec((3, 3 * BR_W), const2),
            pl.BlockSpec((SUBLANES, LANES), const2),
        ],
        out_specs=[
            pl.BlockSpec((None, tm, 6 * BR_W), lambda bi, i: (bi, i, 0)),
            pl.BlockSpec((None, tm, LANES), lambda bi, i: (bi, i, 0)),
        ],
        out_shape=[
            jax.ShapeDtypeStruct((b, tt, 6 * BR_W), BF16),
            jax.ShapeDtypeStruct((b, tt, LANES), F32),
        ],
        compiler_params=pltpu.CompilerParams(vmem_limit_bytes=VMEM_LIMIT),
        name="scan_inputs",
    )(*operands, shift, scale, g_pre, w_scan, cos_tab, sin_tab, conv_w, prm)


def _chunk_cumsum(x, chunk, reverse):
    t = x.shape[0]
    pos = _iota(x.shape, 0) & (chunk - 1)
    s = 1
    while s < chunk:
        if reverse:
            x = x + jnp.where(pos < chunk - s, pltpu.roll(x, t - s, axis=0), 0.0)
        else:
            x = x + jnp.where(pos >= s, pltpu.roll(x, s, axis=0), 0.0)
        s *= 2
    return x


def _expand_exact(x, e):
    hi = x.astype(BF16)
    r1 = x - hi.astype(F32)
    mid = r1.astype(BF16)
    lo = (r1 - mid.astype(F32)).astype(BF16)
    eb = e.astype(BF16)
    return (jnp.dot(hi, eb, preferred_element_type=F32) + jnp.dot(mid, eb, preferred_element_type=F32)
            + jnp.dot(lo, eb, preferred_element_type=F32))


def _retention_chain(q, k, v, s_ref, o_ref, intra, qdec, kdec, cdec, bd_mask):
    scores = _mm_nt(q, _blockdiag(k, RET_CHUNK)) * intra
    yield
    s = s_ref[...]
    o_ref[...] += _mm(scores, _blockdiag(v, RET_CHUNK)) + _mm(q, s) * qdec
    yield
    s_ref[...] = s * cdec + jnp.where(bd_mask, _mm_tn(k * kdec, v), 0.0)


def _unit_triangular_inverse(a, row, col, eye):
    c = GDN_CHUNK
    same16 = (row >> 4) == (col >> 4)
    same32 = (row >> 5) == (col >> 5)
    a16 = jnp.where(same16, a, 0.0)
    p = jnp.where(eye, 1.0, 0.0) - a16
    x = _mm(a16, _blockdiag(a16, c))
    yield
    for _ in range(2):
        px = _mm(jnp.concatenate([p, x], axis=0), _blockdiag(x, c))
        yield
        p = p + px[:c]
        x = px[c:]
    p = p + _mm(p, _blockdiag(x, c))
    yield
    for off in (jnp.where(jnp.logical_and(same32, jnp.logical_not(same16)), a, 0.0),
                jnp.where(same32, 0.0, a)):
        y = _mm(p, _blockdiag(off, c))
        yield
        p = p - _mm(y, _blockdiag(p, c))
        yield
    return p


def _gdn_chunk_operands(q, k, v, gc, beta, reverse):
    c = GDN_CHUNK
    row = _iota((c, BR_W), 0)
    col = _iota((c, BR_W), 1) & (c - 1)
    eye = row == col
    incl = (row <= col) if reverse else (row >= col)
    strict = (row < col) if reverse else (row > col)
    gc_row = jnp.sum(jnp.where(eye, gc, 0.0), axis=0, keepdims=True)
    decay = jnp.exp(jnp.where(incl, gc - gc_row, NEG_INF))
    kk_qk = _mm_nt(jnp.concatenate([k, q], axis=0), _blockdiag(k, c))
    yield
    a = jnp.where(strict, kk_qk[:c] * beta * decay, 0.0)
    inv = yield from _unit_triangular_inverse(a, row, col, eye)
    egc = jnp.exp(gc)
    u = _mm(inv, _blockdiag(v * beta, c))
    w = _mm(inv, _blockdiag(k * beta * egc, c))
    yield
    g_last = gc[0:1, :] if reverse else gc[c - 1:c, :]
    return dict(u=u, w=w, qd=q * egc, intra=kk_qk[c:] * decay, k_tail=k * jnp.exp(g_last - gc),
                cdec=jnp.exp(g_last))


def _gdn_recurrence_chain(chunks, s_ref, bd_mask):
    c = GDN_CHUNK
    s = s_ref[...]
    for ch, o_ref in chunks:
        ws_qs = _mm(jnp.concatenate([ch["w"], ch["qd"]], axis=0), s)
        yield
        v_new = ch["u"] - ws_qs[:c]
        o_ref[...] += ws_qs[c:] + _mm(ch["intra"], _blockdiag(v_new, c))
        s = s * ch["cdec"] + jnp.where(bd_mask, _mm_tn(ch["k_tail"], v_new), 0.0)
        yield
    s_ref[...] = s


def _bwd_block(j, n_steps, n_ctx_steps):
    return jnp.where(j < n_ctx_steps, n_ctx_steps - 1 - j, n_steps - 1 - (j - n_ctx_steps))


def _scan_kernel(qkv_f_ref, qkv_b_ref, gate_f_ref, gate_b_ref, intra_ref, qdec_ref, kdec_ref,
                 cdec_ref, o_ref, s_ret_ref, s_gdn_ref, *, n_steps, n_ctx_steps):
    j = pl.program_id(1)

    @pl.when(j == 0)
    def _():
        s_ret_ref[...] = jnp.zeros_like(s_ret_ref)
        s_gdn_ref[...] = jnp.zeros_like(s_gdn_ref)
        o_ref[...] = jnp.zeros_like(o_ref)

    bd_mask = (_iota((BR_W, BR_W), 0) >> 6) == (_iota((BR_W, BR_W), 1) >> 6)
    erow = _iota((LANES, BR_W), 0)
    ehead = _iota((LANES, BR_W), 1) >> 6
    c = GDN_CHUNK
    n_sub = RET_CHUNK // c
    row0 = (pl.multiple_of(j * RET_CHUNK, RET_CHUNK),
            pl.multiple_of(_bwd_block(j, n_steps, n_ctx_steps) * RET_CHUNK, RET_CHUNK))

    chains, recurrences = [], []
    for bi in range(o_ref.shape[0]):
        for d, (qkv_ref, gate_ref) in enumerate(((qkv_f_ref, gate_f_ref), (qkv_b_ref, gate_b_ref))):
            reverse = d == 1
            qkv = qkv_ref[bi].astype(F32)
            chains.append(_retention_chain(
                qkv[:, 0:BR_W], qkv[:, BR_W:2 * BR_W], qkv[:, 2 * BR_W:3 * BR_W], s_ret_ref.at[bi, d],
                o_ref.at[bi, pl.ds(row0[d], RET_CHUNK), 0:BR_W], intra_ref[d], qdec_ref[d], kdec_ref[d],
                cdec_ref[...], bd_mask))
            gates = gate_ref[bi]
            gc_all = _expand_exact(_chunk_cumsum(gates, c, reverse),
                                   jnp.where(erow == N_HEADS * d + ehead, 1.0, 0.0))
            beta_all = _mm(gates, jnp.where(erow == N_GATES // 2 + N_HEADS * d + ehead, 1.0, 0.0))
            chunk_refs = []
            for sub in (range(n_sub - 1, -1, -1) if reverse else range(n_sub)):
                r0 = sub * c
                chains.append(_gdn_chunk_operands(
                    qkv[r0:r0 + c, 3 * BR_W:4 * BR_W], qkv[r0:r0 + c, 4 * BR_W:5 * BR_W],
                    qkv[r0:r0 + c, 5 * BR_W:6 * BR_W], gc_all[r0:r0 + c], beta_all[r0:r0 + c], reverse))
                chunk_refs.append(o_ref.at[bi, pl.ds(row0[d] + r0, c), BR_W:2 * BR_W])
            recurrences.append((len(chains) - n_sub, chunk_refs, s_gdn_ref.at[bi, d]))

    results = _interleave(chains)
    _interleave([_gdn_recurrence_chain(list(zip(results[first:first + n_sub], refs)), s_ref, bd_mask)
                 for first, refs, s_ref in recurrences])


def _scans(qkv, gates, intra, qdec, kdec, cdec, *, n_ctx_steps, group):
    b, tt, _ = qkv.shape
    c = RET_CHUNK
    ns = tt // c
    fwd = lambda bi, j: (bi, j, 0)
    bwd = lambda bi, j: (bi, _bwd_block(j, ns, n_ctx_steps), 0)
    const3 = lambda bi, j: (0, 0, 0)
    kern = functools.partial(_scan_kernel, n_steps=ns, n_ctx_steps=n_ctx_steps)
    return pl.pallas_call(
        kern,
        grid=(b // group, ns),
        in_specs=[
            pl.BlockSpec((group, c, 6 * BR_W), fwd),
            pl.BlockSpec((group, c, 6 * BR_W), bwd),
            pl.BlockSpec((group, c, LANES), fwd),
            pl.BlockSpec((group, c, LANES), bwd),
            pl.BlockSpec((2, c, N_HEADS * c), const3),
            pl.BlockSpec((2, c, BR_W), const3),
            pl.BlockSpec((2, c, BR_W), const3),
            pl.BlockSpec((1, BR_W), lambda bi, j: (0, 0)),
        ],
        out_specs=pl.BlockSpec((group, tt, 2 * BR_W), lambda bi, j: (bi, 0, 0)),
        out_shape=jax.ShapeDtypeStruct((b, tt, 2 * BR_W), F32),
        scratch_shapes=[
            pltpu.VMEM((group, 2, BR_W, BR_W), F32),
            pltpu.VMEM((group, 2, BR_W, BR_W), F32),
        ],
        compiler_params=pltpu.CompilerParams(
            dimension_semantics=("arbitrary", "arbitrary"), vmem_limit_bytes=VMEM_LIMIT),
        name="bidirectional_scans",
    )(qkv, qkv, gates, gates, intra, qdec, kdec, cdec)


def _finish_sub_block(s, o_ref, xs, prev8, next8, shift_ref, scale_ref, gate_ref, gpre_ref,
                      gpost_ref, wfin_ref, wout_ref, sgw_ref, sgb_ref, scw_ref, rng_ref, gng_ref, out_ref,
                      y_ref, *, n_blocks, n_ctx_blocks, skip):
    rows = slice(s * SUB, (s + 1) * SUB)
    h, has_prev, has_next, is_ctx = _sub_block(s, xs, prev8, next8, shift_ref, scale_ref, gpre_ref,
                                               n_blocks=n_blocks, n_ctx_blocks=n_ctx_blocks, skip=skip)
    ch_ext = jnp.dot(h, wfin_ref[:, 0:2 * BR_W], preferred_element_type=F32)
    pf = jnp.dot(h[:SUB], wfin_ref[:, 2 * BR_W:], preferred_element_type=F32)
    yield
    sc_b, sc_z, ret_z, sg_u, sg_v, sg_z, gdn_z = (pf[:, n * BR_W:(n + 1) * BR_W] for n in range(7))
    hm = _head_mean_matrix()

    o_ret = o_ref[rows, 0:BR_W]
    cen = o_ret - _mm(o_ret, hm)
    var = _mm(cen * cen, hm)
    y_ref[rows, 0:BR_W] = (cen * lax.rsqrt(var + EPS) * rng_ref[...]) * _silu(ret_z)

    u = _gelu_tanh(sg_u)
    v = _gelu_tanh(sg_v)
    vc = v - jnp.mean(v, axis=-1, keepdims=True)
    v = vc * lax.rsqrt(jnp.mean(vc * vc, axis=-1, keepdims=True) + EPS)
    usz = u * _silu(sg_z)
    sgw = sgw_ref[...]
    for r0 in range(0, SUB, RET_CHUNK):
        sg = _mm(sgw, _blockdiag(v[r0:r0 + RET_CHUNK], RET_CHUNK)) + sgb_ref[...]
        y_ref[rows.start + r0:rows.start + r0 + RET_CHUNK, BR_W:2 * BR_W] = usz[r0:r0 + RET_CHUNK] * sg

    conv = _conv3(ch_ext[:, 0:BR_W] * ch_ext[:, BR_W:2 * BR_W], has_prev, has_next, scw_ref[...])
    y_ref[rows, 2 * BR_W:3 * BR_W] = sc_b * conv * _silu(sc_z)

    o_gdn = o_ref[rows, BR_W:2 * BR_W]
    ms = _mm(o_gdn * o_gdn, hm)
    y_ref[rows, 3 * BR_W:4 * BR_W] = (o_gdn * lax.rsqrt(ms + EPS) * gng_ref[...]) * _silu(gdn_z)
    yield

    r = jnp.dot(y_ref[rows, :].astype(BF16), wout_ref[...], preferred_element_type=F32)
    yield
    gate = jnp.where(is_ctx, gate_ref[0], gate_ref[1])
    rn = (r * lax.rsqrt(jnp.mean(r * r, axis=-1, keepdims=True) + EPS)) * gpost_ref[...]
    out_ref[rows, :] = xs[s] + gate * rn


def _finish_kernel(o_ref, *refs, n_sub, n_blocks, n_ctx_blocks, skip, split):
    xs, prev8, next8, rest = _tile_sources(refs, n_sub, n_ctx_blocks, split)
    _interleave([_finish_sub_block(s, o_ref, xs, prev8, next8, *rest, n_blocks=n_blocks,
                                   n_ctx_blocks=n_ctx_blocks, skip=skip) for s in range(n_sub)])


def _finish(o, stream, shift, scale, gate, g_pre, g_post, w_fin, w_out, sgw, sgb, sc_w, ret_g, gdn_g,
            *, tm, n_ctx_blocks, skip):
    specs, operands, mod, b, tt, d = _stream_sources(stream, tm, skip)
    const2 = lambda bi, i: (0, 0)
    kern = functools.partial(_finish_kernel, n_sub=tm // SUB, n_blocks=tt // SUB, n_ctx_blocks=n_ctx_blocks,
                             skip=skip, split=isinstance(stream, tuple))
    return pl.pallas_call(
        kern,
        grid=(b, tt // tm - skip),
        in_specs=[
            pl.BlockSpec((None, tm, 2 * BR_W), lambda bi, i: (bi, i + skip, 0)),
            *specs, mod, mod, mod,
            pl.BlockSpec((1, d), const2),
            pl.BlockSpec((1, d), const2),
            pl.BlockSpec((d, FIN_W), const2),
            pl.BlockSpec((4 * BR_W, d), const2),
            pl.BlockSpec((RET_CHUNK, N_HEADS * RET_CHUNK), const2),
            pl.BlockSpec((RET_CHUNK, BR_W), const2),
            pl.BlockSpec((3, BR_W), const2),
            pl.BlockSpec((1, BR_W), const2),
            pl.BlockSpec((1, BR_W), const2),
        ],
        out_specs=pl.BlockSpec((None, tm, d), lambda bi, i: (bi, i, 0)),
        out_shape=jax.ShapeDtypeStruct((b, tt - skip * tm, d), F32),
        scratch_shapes=[pltpu.VMEM((tm, 4 * BR_W), F32)],
        compiler_params=pltpu.CompilerParams(vmem_limit_bytes=VMEM_LIMIT),
        name="finish",
    )(o, *operands, shift, scale, gate, g_pre, g_post, w_fin, w_out, sgw, sgb, sc_w, ret_g, gdn_g)


def _rope_tables(t_lat, t_ctx):
    nf = HEAD_DIM // 4
    inv = np.float32(ROPE_BASE) ** (-np.arange(nf, dtype=np.float32) / np.float32(nf))
    rows = t_lat // GRID_W
    row = np.repeat(np.arange(rows), GRID_W).astype(np.float32)
    col = np.tile(np.arange(GRID_W), rows).astype(np.float32)
    ang_r = row[:, None] * inv
    ang_c = col[:, None] * inv
    ang = np.concatenate([ang_r, ang_r, ang_c, ang_c], axis=-1)
    cos = np.tile(np.cos(ang), (1, N_HEADS))
    first_half = (np.arange(HEAD_DIM) % (HEAD_DIM // 2)) < nf
    sin = np.tile(np.where(first_half, -np.sin(ang), np.sin(ang)), (1, N_HEADS))
    cos = np.concatenate([np.ones((t_ctx, BR_W), np.float32), cos], axis=0)
    sin = np.concatenate([np.zeros((t_ctx, BR_W), np.float32), sin], axis=0)
    return jnp.asarray(cos, F32), jnp.asarray(sin, F32)


def _retention_tables():
    c = RET_CHUNK
    log_gamma = np.log(1.0 - 2.0 ** (-5.0 - np.arange(N_HEADS, dtype=np.float32))).astype(np.float32)
    pos = np.arange(c, dtype=np.float32)
    lg = log_gamma[:, None]
    diff = pos[:, None] - pos[None, :]
    intra_f = np.exp(np.where(diff >= 0, diff * lg[..., None], -np.inf))
    intra_b = np.swapaxes(intra_f, 1, 2)
    wide = lambda m: np.transpose(m, (1, 0, 2)).reshape(c, N_HEADS * c)
    nat = lambda m: np.repeat(m.T, HEAD_DIM, axis=1)
    q_f = np.exp((pos + 1.0) * lg)
    k_f = np.exp((c - 1.0 - pos) * lg)
    intra = np.stack([wide(intra_f), wide(intra_b)])
    qdec = np.stack([nat(q_f), nat(q_f[:, ::-1])])
    kdec = np.stack([nat(k_f), nat(k_f[:, ::-1])])
    cdec = np.repeat(np.exp(c * log_gamma), HEAD_DIM)[None, :]
    return tuple(jnp.asarray(t, F32) for t in (intra, qdec, kdec, cdec))


def _split_w_in(w_in):
    depth, d, _ = w_in.shape
    wb = w_in.astype(BF16)
    seg = lambda n: wb[:, :, n * BR_W:(n + 1) * BR_W]
    pad = jnp.zeros((depth, d, LANES - N_GATES), BF16)
    w_scan = jnp.concatenate([seg(0), seg(1), seg(2), seg(11), seg(12), seg(13), wb[:, :, 15 * BR_W:], pad], axis=-1)
    w_fin = jnp.concatenate([seg(8), seg(9), seg(7), seg(10), seg(3), seg(4), seg(5), seg(6), seg(14)], axis=-1)
    return w_scan, w_fin


def _token_tile(tt):
    n_blocks = tt // SUB
    return SUB * max(n for n in (1, 2, 3) if n_blocks % n == 0)


def kernel(x, c, ctx, c_ctx, w_mod, b_mod, g_pre, g_post, w_in, w_out, ret_norm_g, sg_w, sg_b,
           sc_conv_w, gdn_conv_w, gdn_a_log, gdn_dt_bias, gdn_norm_g):
    b, t_lat, d = x.shape
    t_ctx = ctx.shape[1]
    depth = w_mod.shape[0]
    assert t_ctx % SUB == 0 and t_lat % SUB == 0 and SUB % RET_CHUNK == 0 and t_lat % GRID_W == 0
    tm = _token_tile(t_ctx + t_lat)
    n_ctx_blocks = t_ctx // SUB
    group = 2 if b % 2 == 0 else 1

    bp = -(-(b + 1) // SUBLANES) * SUBLANES
    cc = jnp.concatenate([c, c_ctx[None, :], jnp.zeros((bp - b - 1, d), F32)], axis=0)
    mod = _modulation(cc, w_mod, b_mod)
    mod = jnp.stack([jnp.broadcast_to(mod[:, :, b:b + 1], (depth, 3, b, d)), mod[:, :, :b]], axis=3)
    mod = mod[:, :, :, :, None, :]

    cos_tab, sin_tab = _rope_tables(t_lat, t_ctx)
    intra, qdec, kdec, cdec = _retention_tables()
    w_scan, w_fin = _split_w_in(w_in)
    w_out_b = w_out.astype(BF16)
    prm = jnp.zeros((depth, SUBLANES, LANES), F32)
    prm = prm.at[:, 0, :N_GATES // 2].set(gdn_a_log.reshape(depth, -1))
    prm = prm.at[:, 1, :N_GATES // 2].set(gdn_dt_bias.reshape(depth, -1))
    sgw = jnp.transpose(sg_w, (0, 2, 1, 3)).reshape(depth, RET_CHUNK, N_HEADS * RET_CHUNK)
    sgb = jnp.repeat(jnp.swapaxes(sg_b, 1, 2), HEAD_DIM, axis=2)
    gdn_g = jnp.tile(gdn_norm_g, (1, N_HEADS))

    xc = (ctx, x)
    for l in range(depth):
        last = l == depth - 1
        shift, scale, gate = mod[l, 0], mod[l, 1], mod[l, 2]
        qkv, gates = _scan_inputs(xc, shift, scale, g_pre[l][None, :], w_scan[l], cos_tab, sin_tab,
                                  gdn_conv_w[l], prm[l], tm=tm, n_ctx_blocks=n_ctx_blocks)
        o = _scans(qkv, gates, intra, qdec, kdec, cdec, n_ctx_steps=t_ctx // RET_CHUNK, group=group)
        xc = _finish(o, xc, shift, scale, gate, g_pre[l][None, :], g_post[l][None, :], w_fin[l], w_out_b[l],
                     sgw[l], sgb[l], sc_conv_w[l], ret_norm_g[l][None, :], gdn_g[l][None, :],
                     tm=SUB if last else tm, n_ctx_blocks=n_ctx_blocks, skip=n_ctx_blocks if last else 0)
    return xc
```

```python
import functools

import jax
import jax.numpy as jnp
import numpy as np
from jax import lax
from jax.experimental import pallas as pl
from jax.experimental.pallas import tpu as pltpu

HEAD_DIM = 64
N_HEADS = 4
BR_W = N_HEADS * HEAD_DIM
GRID_W = 64
RET_CHUNK = 128
GDN_CHUNK = 64
ROPE_BASE = 10000.0
EPS = 1e-6
N_GATES = 16
LANES = 128
SUBLANES = 8
SUB = 256
SCAN_SEGMENTS = (0, 1, 2, 11, 12, 13)
FIN_SEGMENTS = (8, 9, 7, 10, 3, 4, 5, 6, 14)
VMEM_LIMIT = 56 * 1024 * 1024

F32 = jnp.float32
BF16 = jnp.bfloat16
NEG_INF = float("-inf")


def _mm(a, b):
    return jnp.dot(a.astype(BF16), b.astype(BF16), preferred_element_type=F32)


def _mm_nt(a, b):
    return lax.dot_general(a.astype(BF16), b.astype(BF16), (((1,), (1,)), ((), ())),
                           preferred_element_type=F32)


def _mm_tn(a, b):
    return lax.dot_general(a.astype(BF16), b.astype(BF16), (((0,), (0,)), ((), ())),
                           preferred_element_type=F32)


def _iota(shape, dim):
    return lax.broadcasted_iota(jnp.int32, shape, dim)


def _silu(x):
    return x * jax.nn.sigmoid(x)


def _gelu_tanh(x):
    return 0.5 * x * (1.0 + jnp.tanh(0.7978845608028654 * (x + 0.044715 * (x * x * x))))


def _head_mean_matrix():
    r = _iota((BR_W, BR_W), 0) >> 6
    c = _iota((BR_W, BR_W), 1) >> 6
    return jnp.where(r == c, 1.0 / HEAD_DIM, 0.0).astype(F32)


def _blockdiag(x, chunk):
    xb = x.astype(BF16)
    zeros = jnp.zeros((chunk, LANES), BF16)
    low_head = _iota((chunk, LANES), 1) < HEAD_DIM
    blocks = []
    for head in range(N_HEADS):
        half = xb[:, (head // 2) * LANES:(head // 2 + 1) * LANES]
        kept = jnp.where(low_head if head % 2 == 0 else jnp.logical_not(low_head), half, zeros)
        blocks.append(jnp.concatenate([kept, zeros] if head < 2 else [zeros, kept], axis=1))
    return jnp.concatenate(blocks, axis=0)


def _interleave(chains):
    results = [None] * len(chains)
    live = list(enumerate(chains))
    while live:
        still = []
        for idx, chain in live:
            try:
                next(chain)
                still.append((idx, chain))
            except StopIteration as stop:
                results[idx] = stop.value
        live = still
    return results


def _token_tile_specs(tm, d, tt, skip=0):
    hb = tm // SUBLANES
    tile = pl.BlockSpec((None, tm, d), lambda bi, i: (bi, i + skip, 0))
    prev = pl.BlockSpec((None, SUBLANES, d), lambda bi, i: (bi, jnp.maximum((i + skip) * hb - 1, 0), 0))
    nxt = pl.BlockSpec((None, SUBLANES, d),
                       lambda bi, i: (bi, jnp.minimum((i + skip + 1) * hb, tt // SUBLANES - 1), 0))
    mod = pl.BlockSpec((None, 2, 1, d), lambda bi, i: (bi, 0, 0, 0))
    return [tile, prev, nxt], mod


def _split_tile_specs(tm, d, t_ctx, t_lat):
    n_sub, n_ctx, hb = tm // SUB, t_ctx // SUB, SUB // SUBLANES

    def rows8(first, t):
        return pl.BlockSpec((None, SUBLANES, d),
                            lambda bi, i: (bi, jnp.clip(first(i), 0, t // SUBLANES - 1), 0))

    ctx_blocks = [pl.BlockSpec((None, SUB, d), lambda bi, i, s=s: (bi, jnp.minimum(i * n_sub + s, n_ctx - 1), 0))
                  for s in range(n_sub)]
    lat_blocks = [pl.BlockSpec((None, SUB, d), lambda bi, i, s=s: (bi, jnp.maximum(i * n_sub + s - n_ctx, 0), 0))
                  for s in range(n_sub)]
    halos = [rows8(lambda i: i * n_sub * hb - 1, t_ctx), rows8(lambda i: (i * n_sub - n_ctx) * hb - 1, t_lat),
             rows8(lambda i: (i + 1) * n_sub * hb, t_ctx), rows8(lambda i: ((i + 1) * n_sub - n_ctx) * hb, t_lat)]
    return ctx_blocks + lat_blocks + halos


def _stream_sources(stream, tm, skip=0):
    if isinstance(stream, tuple):
        ctx, lat = stream
        (b, t_ctx, d), t_lat = ctx.shape, lat.shape[1]
        n_sub = tm // SUB
        specs = _split_tile_specs(tm, d, t_ctx, t_lat)
        operands = [ctx] * n_sub + [lat] * n_sub + [ctx, lat, ctx, lat]
        mod = pl.BlockSpec((None, 2, 1, d), lambda bi, i: (bi, 0, 0, 0))
        return specs, operands, mod, b, t_ctx + t_lat, d
    b, tt, d = stream.shape
    specs, mod = _token_tile_specs(tm, d, tt, skip)
    return specs, [stream] * 3, mod, b, tt, d


def _tile_sources(refs, n_sub, n_ctx_blocks, split):
    if not split:
        x_ref, prev_ref, next_ref = refs[:3]
        return ([x_ref[s * SUB:(s + 1) * SUB, :] for s in range(n_sub)], prev_ref[...], next_ref[...],
                refs[3:])
    ctx_refs, lat_refs = refs[:n_sub], refs[n_sub:2 * n_sub]
    ctx_prev, lat_prev, ctx_next, lat_next = refs[2 * n_sub:2 * n_sub + 4]
    blk0 = pl.program_id(1) * n_sub
    xs = [jnp.where(blk0 + s < n_ctx_blocks, ctx_refs[s][...], lat_refs[s][...]) for s in range(n_sub)]
    prev8 = jnp.where(blk0 - 1 < n_ctx_blocks, ctx_prev[...], lat_prev[...])
    next8 = jnp.where(blk0 + n_sub < n_ctx_blocks, ctx_next[...], lat_next[...])
    return xs, prev8, next8, refs[2 * n_sub + 4:]


def _sub_block(s, xs, prev8, next8, shift_ref, scale_ref, g_ref, *, n_blocks, n_ctx_blocks, skip=0):
    n_sub = len(xs)
    blk = (pl.program_id(1) + skip) * n_sub + s
    before = prev8 if s == 0 else xs[s - 1][SUB - SUBLANES:, :]
    after = next8 if s == n_sub - 1 else xs[s + 1][:SUBLANES, :]
    x = jnp.concatenate([xs[s], before, after], axis=0)
    is_ctx = blk < n_ctx_blocks
    scale = jnp.where(is_ctx, scale_ref[0], scale_ref[1])
    shift = jnp.where(is_ctx, shift_ref[0], shift_ref[1])
    gain = g_ref[...] * (1.0 + scale)
    h = (x * lax.rsqrt(jnp.mean(x * x, axis=-1, keepdims=True) + EPS) * gain + shift).astype(BF16)
    has_prev = jnp.logical_and(blk > 0, blk != n_ctx_blocks)
    has_next = jnp.logical_and(blk < n_blocks - 1, blk != n_ctx_blocks - 1)
    return h, has_prev, has_next, is_ctx


def _conv3(ext, has_prev, has_next, w):
    x = ext[:SUB]
    prev_row = jnp.where(has_prev, ext[SUB + SUBLANES - 1:SUB + SUBLANES, :], 0.0)
    next_row = jnp.where(has_next, ext[SUB + SUBLANES:SUB + SUBLANES + 1, :], 0.0)
    row = _iota(x.shape, 0)
    x_prev = jnp.where(row == 0, prev_row, pltpu.roll(x, 1, axis=0))
    x_next = jnp.where(row == SUB - 1, next_row, pltpu.roll(x, SUB - 1, axis=0))
    return x_prev * w[0:1, :] + x * w[1:2, :] + x_next * w[2:3, :]


def _mod_kernel(cc_ref, w_ref, b_ref, o_ref):
    s = _silu(cc_ref[...])
    o_ref[...] = _mm(s, w_ref[...]) + b_ref[...]


def _modulation(cc, w_mod, b_mod):
    depth, d, _ = w_mod.shape
    bp = cc.shape[0]
    return pl.pallas_call(
        _mod_kernel,
        grid=(depth, 3),
        in_specs=[
            pl.BlockSpec((bp, d), lambda l, n: (0, 0)),
            pl.BlockSpec((None, d, d), lambda l, n: (l, 0, n)),
            pl.BlockSpec((None, None, 1, d), lambda l, n: (l, n, 0, 0)),
        ],
        out_specs=pl.BlockSpec((None, None, bp, d), lambda l, n: (l, n, 0, 0)),
        out_shape=jax.ShapeDtypeStruct((depth, 3, bp, d), F32),
        compiler_params=pltpu.CompilerParams(vmem_limit_bytes=VMEM_LIMIT),
        name="modulation",
    )(cc, w_mod, b_mod.reshape(depth, 3, 1, d))


def _scan_inputs_sub_block(s, xs, prev8, next8, shift_ref, scale_ref, g_ref, wq_ref, wk_ref, wv_ref, gq_ref,
                           gk_ref, gv_ref, wgate_ref, cos_ref, sin_ref, cw_ref, prm_ref, qkv_ref, gate_ref,
                           *, n_blocks, n_ctx_blocks):
    rows = slice(s * SUB, (s + 1) * SUB)
    h, has_prev, has_next, _ = _sub_block(s, xs, prev8, next8, shift_ref, scale_ref, g_ref,
                                          n_blocks=n_blocks, n_ctx_blocks=n_ctx_blocks)
    ret = jnp.concatenate([jnp.dot(h[:SUB], w[...], preferred_element_type=F32)
                           for w in (wq_ref, wk_ref, wv_ref)], axis=1)
    gdn_ext = jnp.concatenate([jnp.dot(h, w[...], preferred_element_type=F32)
                               for w in (gq_ref, gk_ref, gv_ref)], axis=1)
    a = jnp.dot(h[:SUB], wgate_ref[...], preferred_element_type=F32)
    yield

    cos = cos_ref[rows, :]
    sin = sin_ref[rows, :]
    first_half = (_iota(cos.shape, 1) & 31) < 16

    def rope(x):
        partner = jnp.where(first_half, pltpu.roll(x, BR_W - 16, axis=1), pltpu.roll(x, 16, axis=1))
        return x * cos + partner * sin

    qkv_ref[rows, 0 * BR_W:1 * BR_W] = rope(ret[:, 0 * BR_W:1 * BR_W]).astype(BF16)
    qkv_ref[rows, 1 * BR_W:2 * BR_W] = (rope(ret[:, 1 * BR_W:2 * BR_W]) * HEAD_DIM ** -0.5).astype(BF16)
    qkv_ref[rows, 2 * BR_W:3 * BR_W] = ret[:, 2 * BR_W:3 * BR_W].astype(BF16)

    g = _silu(_conv3(gdn_ext, has_prev, has_next, cw_ref[...]))
    hsum = _head_mean_matrix() * HEAD_DIM

    def l2n(x):
        return x * lax.rsqrt(_mm(x * x, hsum) + EPS)

    qkv_ref[rows, 3 * BR_W:4 * BR_W] = (l2n(g[:, 0:BR_W]) * HEAD_DIM ** -0.5).astype(BF16)
    qkv_ref[rows, 4 * BR_W:5 * BR_W] = l2n(g[:, BR_W:2 * BR_W]).astype(BF16)
    qkv_ref[rows, 5 * BR_W:6 * BR_W] = g[:, 2 * BR_W:3 * BR_W].astype(BF16)

    neg_rate = -jnp.exp(prm_ref[0:1, :])
    z = a + prm_ref[1:2, :]
    softplus = jnp.maximum(z, 0.0) + jnp.log1p(jnp.exp(-jnp.abs(z)))
    gate_ref[rows, :] = jnp.where(_iota(a.shape, 1) < N_GATES // 2, neg_rate * softplus, jax.nn.sigmoid(a))


def _scan_inputs_kernel(*refs, n_sub, n_blocks, n_ctx_blocks, split):
    xs, prev8, next8, rest = _tile_sources(refs, n_sub, n_ctx_blocks, split)
    _interleave([_scan_inputs_sub_block(s, xs, prev8, next8, *rest, n_blocks=n_blocks, n_ctx_blocks=n_ctx_blocks)
                 for s in range(n_sub)])


def _segment_specs(layer, d, segments):
    return [pl.BlockSpec((None, d, BR_W), lambda bi, i, n=n: (layer, 0, n)) for n in segments]


def _scan_inputs(stream, shift, scale, g_pre, w_in, w_gates, layer, cos_tab, sin_tab, conv_w, prm,
                 *, tm, n_ctx_blocks):
    specs, operands, mod, b, tt, d = _stream_sources(stream, tm)
    const2 = lambda bi, i: (0, 0)
    kern = functools.partial(_scan_inputs_kernel, n_sub=tm // SUB, n_blocks=tt // SUB,
                             n_ctx_blocks=n_ctx_blocks, split=isinstance(stream, tuple))
    return pl.pallas_call(
        kern,
        grid=(b, tt // tm),
        in_specs=[
            *specs, mod, mod,
            pl.BlockSpec((1, d), const2),
            *_segment_specs(layer, d, SCAN_SEGMENTS),
            pl.BlockSpec((None, d, LANES), lambda bi, i: (layer, 0, 0)),
            pl.BlockSpec((tm, BR_W), lambda bi, i: (i, 0)),
            pl.BlockSpec((tm, BR_W), lambda bi, i: (i, 0)),
            pl.BlockSpec((3, 3 * BR_W), const2),
            pl.BlockSpec((SUBLANES, LANES), const2),
        ],
        out_specs=[
            pl.BlockSpec((None, tm, 6 * BR_W), lambda bi, i: (bi, i, 0)),
            pl.BlockSpec((None, tm, LANES), lambda bi, i: (bi, i, 0)),
        ],
        out_shape=[
            jax.ShapeDtypeStruct((b, tt, 6 * BR_W), BF16),
            jax.ShapeDtypeStruct((b, tt, LANES), F32),
        ],
        compiler_params=pltpu.CompilerParams(vmem_limit_bytes=VMEM_LIMIT),
        name="scan_inputs",
    )(*operands, shift, scale, g_pre, *[w_in] * len(SCAN_SEGMENTS), w_gates, cos_tab, sin_tab, conv_w, prm)


def _chunk_cumsum(x, chunk, reverse):
    t = x.shape[0]
    pos = _iota(x.shape, 0) & (chunk - 1)
    s = 1
    while s < chunk:
        if reverse:
            x = x + jnp.where(pos < chunk - s, pltpu.roll(x, t - s, axis=0), 0.0)
        else:
            x = x + jnp.where(pos >= s, pltpu.roll(x, s, axis=0), 0.0)
        s *= 2
    return x


def _expand_exact(x, e):
    hi = x.astype(BF16)
    r1 = x - hi.astype(F32)
    mid = r1.astype(BF16)
    lo = (r1 - mid.astype(F32)).astype(BF16)
    eb = e.astype(BF16)
    return (jnp.dot(hi, eb, preferred_element_type=F32) + jnp.dot(mid, eb, preferred_element_type=F32)
            + jnp.dot(lo, eb, preferred_element_type=F32))


def _retention_chain(q, k, v, s_ref, o_ref, intra, qdec, kdec, cdec, bd_mask):
    scores = _mm_nt(q, _blockdiag(k, RET_CHUNK)) * intra
    yield
    s = s_ref[...]
    o_ref[...] += _mm(scores, _blockdiag(v, RET_CHUNK)) + _mm(q, s) * qdec
    yield
    s_ref[...] = s * cdec + jnp.where(bd_mask, _mm_tn(k * kdec, v), 0.0)


def _unit_triangular_inverse(a, row, col, eye):
    c = GDN_CHUNK
    same16 = (row >> 4) == (col >> 4)
    same32 = (row >> 5) == (col >> 5)
    a16 = jnp.where(same16, a, 0.0)
    p = jnp.where(eye, 1.0, 0.0) - a16
    x = _mm(a16, _blockdiag(a16, c))
    yield
    for _ in range(2):
        px = _mm(jnp.concatenate([p, x], axis=0), _blockdiag(x, c))
        yield
        p = p + px[:c]
        x = px[c:]
    p = p + _mm(p, _blockdiag(x, c))
    yield
    for off in (jnp.where(jnp.logical_and(same32, jnp.logical_not(same16)), a, 0.0),
                jnp.where(same32, 0.0, a)):
        y = _mm(p, _blockdiag(off, c))
        yield
        p = p - _mm(y, _blockdiag(p, c))
        yield
    return p


def _gdn_chunk_operands(q, k, v, gc, beta, reverse):
    c = GDN_CHUNK
    row = _iota((c, BR_W), 0)
    col = _iota((c, BR_W), 1) & (c - 1)
    eye = row == col
    incl = (row <= col) if reverse else (row >= col)
    strict = (row < col) if reverse else (row > col)
    gc_row = jnp.sum(jnp.where(eye, gc, 0.0), axis=0, keepdims=True)
    decay = jnp.exp(jnp.where(incl, gc - gc_row, NEG_INF))
    kk_qk = _mm_nt(jnp.concatenate([k, q], axis=0), _blockdiag(k, c))
    yield
    a = jnp.where(strict, kk_qk[:c] * beta * decay, 0.0)
    inv = yield from _unit_triangular_inverse(a, row, col, eye)
    egc = jnp.exp(gc)
    u = _mm(inv, _blockdiag(v * beta, c))
    w = _mm(inv, _blockdiag(k * beta * egc, c))
    yield
    g_last = gc[0:1, :] if reverse else gc[c - 1:c, :]
    return dict(u=u, w=w, qd=q * egc, intra=kk_qk[c:] * decay, k_tail=k * jnp.exp(g_last - gc),
                cdec=jnp.exp(g_last))


def _gdn_recurrence_chain(chunks, s_ref, bd_mask):
    c = GDN_CHUNK
    s = s_ref[...]
    for ch, o_ref in chunks:
        ws_qs = _mm(jnp.concatenate([ch["w"], ch["qd"]], axis=0), s)
        yield
        v_new = ch["u"] - ws_qs[:c]
        o_ref[...] += ws_qs[c:] + _mm(ch["intra"], _blockdiag(v_new, c))
        s = s * ch["cdec"] + jnp.where(bd_mask, _mm_tn(ch["k_tail"], v_new), 0.0)
        yield
    s_ref[...] = s


def _bwd_block(j, n_steps, n_ctx_steps):
    return jnp.where(j < n_ctx_steps, n_ctx_steps - 1 - j, n_steps - 1 - (j - n_ctx_steps))


def _scan_kernel(qkv_f_ref, qkv_b_ref, gate_f_ref, gate_b_ref, intra_ref, qdec_ref, kdec_ref,
                 cdec_ref, o_ref, s_ret_ref, s_gdn_ref, *, n_steps, n_ctx_steps):
    j = pl.program_id(1)

    @pl.when(j == 0)
    def _():
        s_ret_ref[...] = jnp.zeros_like(s_ret_ref)
        s_gdn_ref[...] = jnp.zeros_like(s_gdn_ref)
        o_ref[...] = jnp.zeros_like(o_ref)

    bd_mask = (_iota((BR_W, BR_W), 0) >> 6) == (_iota((BR_W, BR_W), 1) >> 6)
    erow = _iota((LANES, BR_W), 0)
    ehead = _iota((LANES, BR_W), 1) >> 6
    c = GDN_CHUNK
    n_sub = RET_CHUNK // c
    row0 = (pl.multiple_of(j * RET_CHUNK, RET_CHUNK),
            pl.multiple_of(_bwd_block(j, n_steps, n_ctx_steps) * RET_CHUNK, RET_CHUNK))

    chains, recurrences = [], []
    for bi in range(o_ref.shape[0]):
        for d, (qkv_ref, gate_ref) in enumerate(((qkv_f_ref, gate_f_ref), (qkv_b_ref, gate_b_ref))):
            reverse = d == 1
            qkv = qkv_ref[bi].astype(F32)
            chains.append(_retention_chain(
                qkv[:, 0:BR_W], qkv[:, BR_W:2 * BR_W], qkv[:, 2 * BR_W:3 * BR_W], s_ret_ref.at[bi, d],
                o_ref.at[bi, pl.ds(row0[d], RET_CHUNK), 0:BR_W], intra_ref[d], qdec_ref[d], kdec_ref[d],
                cdec_ref[...], bd_mask))
            gates = gate_ref[bi]
            gc_all = _expand_exact(_chunk_cumsum(gates, c, reverse),
                                   jnp.where(erow == N_HEADS * d + ehead, 1.0, 0.0))
            beta_all = _mm(gates, jnp.where(erow == N_GATES // 2 + N_HEADS * d + ehead, 1.0, 0.0))
            chunk_refs = []
            for sub in (range(n_sub - 1, -1, -1) if reverse else range(n_sub)):
                r0 = sub * c
                chains.append(_gdn_chunk_operands(
                    qkv[r0:r0 + c, 3 * BR_W:4 * BR_W], qkv[r0:r0 + c, 4 * BR_W:5 * BR_W],
                    qkv[r0:r0 + c, 5 * BR_W:6 * BR_W], gc_all[r0:r0 + c], beta_all[r0:r0 + c], reverse))
                chunk_refs.append(o_ref.at[bi, pl.ds(row0[d] + r0, c), BR_W:2 * BR_W])
            recurrences.append((len(chains) - n_sub, chunk_refs, s_gdn_ref.at[bi, d]))

    results = _interleave(chains)
    _interleave([_gdn_recurrence_chain(list(zip(results[first:first + n_sub], refs)), s_ref, bd_mask)
                 for first, refs, s_ref in recurrences])


def _scans(qkv, gates, intra, qdec, kdec, cdec, *, n_ctx_steps, group):
    b, tt, _ = qkv.shape
    c = RET_CHUNK
    ns = tt // c
    fwd = lambda bi, j: (bi, j, 0)
    bwd = lambda bi, j: (bi, _bwd_block(j, ns, n_ctx_steps), 0)
    const3 = lambda bi, j: (0, 0, 0)
    kern = functools.partial(_scan_kernel, n_steps=ns, n_ctx_steps=n_ctx_steps)
    return pl.pallas_call(
        kern,
        grid=(b // group, ns),
        in_specs=[
            pl.BlockSpec((group, c, 6 * BR_W), fwd),
            pl.BlockSpec((group, c, 6 * BR_W), bwd),
            pl.BlockSpec((group, c, LANES), fwd),
            pl.BlockSpec((group, c, LANES), bwd),
            pl.BlockSpec((2, c, N_HEADS * c), const3),
            pl.BlockSpec((2, c, BR_W), const3),
            pl.BlockSpec((2, c, BR_W), const3),
            pl.BlockSpec((1, BR_W), lambda bi, j: (0, 0)),
        ],
        out_specs=pl.BlockSpec((group, tt, 2 * BR_W), lambda bi, j: (bi, 0, 0)),
        out_shape=jax.ShapeDtypeStruct((b, tt, 2 * BR_W), F32),
        scratch_shapes=[
            pltpu.VMEM((group, 2, BR_W, BR_W), F32),
            pltpu.VMEM((group, 2, BR_W, BR_W), F32),
        ],
        compiler_params=pltpu.CompilerParams(
            dimension_semantics=("arbitrary", "arbitrary"), vmem_limit_bytes=VMEM_LIMIT),
        name="bidirectional_scans",
    )(qkv, qkv, gates, gates, intra, qdec, kdec, cdec)


def _finish_sub_block(s, o_ref, xs, prev8, next8, shift_ref, scale_ref, gate_ref, gpre_ref,
                      gpost_ref, wc_ref, wh_ref, *refs, n_blocks, n_ctx_blocks, skip):
    wfin_refs = refs[:len(FIN_SEGMENTS) - 2]
    wout_ref, sgw_ref, sgb_ref, scw_ref, rng_ref, gng_ref, out_ref, y_ref = refs[len(FIN_SEGMENTS) - 2:]
    rows = slice(s * SUB, (s + 1) * SUB)
    h, has_prev, has_next, is_ctx = _sub_block(s, xs, prev8, next8, shift_ref, scale_ref, gpre_ref,
                                               n_blocks=n_blocks, n_ctx_blocks=n_ctx_blocks, skip=skip)
    ch_ext = (jnp.dot(h, wc_ref[...], preferred_element_type=F32)
              * jnp.dot(h, wh_ref[...], preferred_element_type=F32))
    pf = jnp.concatenate([jnp.dot(h[:SUB], w[...], preferred_element_type=F32) for w in wfin_refs], axis=1)
    yield
    sc_b, sc_z, ret_z, sg_u, sg_v, sg_z, gdn_z = (pf[:, n * BR_W:(n + 1) * BR_W] for n in range(7))
    hm = _head_mean_matrix()

    o_ret = o_ref[rows, 0:BR_W]
    cen = o_ret - _mm(o_ret, hm)
    var = _mm(cen * cen, hm)
    y_ref[rows, 0:BR_W] = (cen * lax.rsqrt(var + EPS) * rng_ref[...]) * _silu(ret_z)

    u = _gelu_tanh(sg_u)
    v = _gelu_tanh(sg_v)
    vc = v - jnp.mean(v, axis=-1, keepdims=True)
    v = vc * lax.rsqrt(jnp.mean(vc * vc, axis=-1, keepdims=True) + EPS)
    usz = u * _silu(sg_z)
    sgw = sgw_ref[...]
    for r0 in range(0, SUB, RET_CHUNK):
        sg = _mm(sgw, _blockdiag(v[r0:r0 + RET_CHUNK], RET_CHUNK)) + sgb_ref[...]
        y_ref[rows.start + r0:rows.start + r0 + RET_CHUNK, BR_W:2 * BR_W] = usz[r0:r0 + RET_CHUNK] * sg

    conv = _conv3(ch_ext, has_prev, has_next, scw_ref[...])
    y_ref[rows, 2 * BR_W:3 * BR_W] = sc_b * conv * _silu(sc_z)

    o_gdn = o_ref[rows, BR_W:2 * BR_W]
    ms = _mm(o_gdn * o_gdn, hm)
    y_ref[rows, 3 * BR_W:4 * BR_W] = (o_gdn * lax.rsqrt(ms + EPS) * gng_ref[...]) * _silu(gdn_z)
    yield

    r = jnp.dot(y_ref[rows, :].astype(BF16), wout_ref[...], preferred_element_type=F32)
    yield
    gate = jnp.where(is_ctx, gate_ref[0], gate_ref[1])
    rn = (r * lax.rsqrt(jnp.mean(r * r, axis=-1, keepdims=True) + EPS)) * gpost_ref[...]
    out_ref[rows, :] = xs[s] + gate * rn


def _finish_kernel(o_ref, *refs, n_sub, n_blocks, n_ctx_blocks, skip, split):
    xs, prev8, next8, rest = _tile_sources(refs, n_sub, n_ctx_blocks, split)
    _interleave([_finish_sub_block(s, o_ref, xs, prev8, next8, *rest, n_blocks=n_blocks,
                                   n_ctx_blocks=n_ctx_blocks, skip=skip) for s in range(n_sub)])


def _finish(o, stream, shift, scale, gate, g_pre, g_post, w_in, w_out, layer, sgw, sgb, sc_w, ret_g, gdn_g,
            *, tm, n_ctx_blocks, skip):
    specs, operands, mod, b, tt, d = _stream_sources(stream, tm, skip)
    const2 = lambda bi, i: (0, 0)
    kern = functools.partial(_finish_kernel, n_sub=tm // SUB, n_blocks=tt // SUB, n_ctx_blocks=n_ctx_blocks,
                             skip=skip, split=isinstance(stream, tuple))
    return pl.pallas_call(
        kern,
        grid=(b, tt // tm - skip),
        in_specs=[
            pl.BlockSpec((None, tm, 2 * BR_W), lambda bi, i: (bi, i + skip, 0)),
            *specs, mod, mod, mod,
            pl.BlockSpec((1, d), const2),
            pl.BlockSpec((1, d), const2),
            *_segment_specs(layer, d, FIN_SEGMENTS),
            pl.BlockSpec((None, 4 * BR_W, d), lambda bi, i: (layer, 0, 0)),
            pl.BlockSpec((RET_CHUNK, N_HEADS * RET_CHUNK), const2),
            pl.BlockSpec((RET_CHUNK, BR_W), const2),
            pl.BlockSpec((3, BR_W), const2),
            pl.BlockSpec((1, BR_W), const2),
            pl.BlockSpec((1, BR_W), const2),
        ],
        out_specs=pl.BlockSpec((None, tm, d), lambda bi, i: (bi, i, 0)),
        out_shape=jax.ShapeDtypeStruct((b, tt - skip * tm, d), F32),
        scratch_shapes=[pltpu.VMEM((tm, 4 * BR_W), F32)],
        compiler_params=pltpu.CompilerParams(vmem_limit_bytes=VMEM_LIMIT),
        name="finish",
    )(o, *operands, shift, scale, gate, g_pre, g_post, *[w_in] * len(FIN_SEGMENTS), w_out, sgw, sgb, sc_w,
      ret_g, gdn_g)


def _rope_tables(t_lat, t_ctx):
    nf = HEAD_DIM // 4
    inv = np.float32(ROPE_BASE) ** (-np.arange(nf, dtype=np.float32) / np.float32(nf))
    rows = t_lat // GRID_W
    row = np.repeat(np.arange(rows), GRID_W).astype(np.float32)
    col = np.tile(np.arange(GRID_W), rows).astype(np.float32)
    ang_r = row[:, None] * inv
    ang_c = col[:, None] * inv
    ang = np.concatenate([ang_r, ang_r, ang_c, ang_c], axis=-1)
    cos = np.tile(np.cos(ang), (1, N_HEADS))
    first_half = (np.arange(HEAD_DIM) % (HEAD_DIM // 2)) < nf
    sin = np.tile(np.where(first_half, -np.sin(ang), np.sin(ang)), (1, N_HEADS))
    cos = np.concatenate([np.ones((t_ctx, BR_W), np.float32), cos], axis=0)
    sin = np.concatenate([np.zeros((t_ctx, BR_W), np.float32), sin], axis=0)
    return jnp.asarray(cos, F32), jnp.asarray(sin, F32)


def _retention_tables():
    c = RET_CHUNK
    log_gamma = np.log(1.0 - 2.0 ** (-5.0 - np.arange(N_HEADS, dtype=np.float32))).astype(np.float32)
    pos = np.arange(c, dtype=np.float32)
    lg = log_gamma[:, None]
    diff = pos[:, None] - pos[None, :]
    intra_f = np.exp(np.where(diff >= 0, diff * lg[..., None], -np.inf))
    intra_b = np.swapaxes(intra_f, 1, 2)
    wide = lambda m: np.transpose(m, (1, 0, 2)).reshape(c, N_HEADS * c)
    nat = lambda m: np.repeat(m.T, HEAD_DIM, axis=1)
    q_f = np.exp((pos + 1.0) * lg)
    k_f = np.exp((c - 1.0 - pos) * lg)
    intra = np.stack([wide(intra_f), wide(intra_b)])
    qdec = np.stack([nat(q_f), nat(q_f[:, ::-1])])
    kdec = np.stack([nat(k_f), nat(k_f[:, ::-1])])
    cdec = np.repeat(np.exp(c * log_gamma), HEAD_DIM)[None, :]
    return tuple(jnp.asarray(t, F32) for t in (intra, qdec, kdec, cdec))


def _gate_weights(wb):
    depth, d, _ = wb.shape
    gates = wb[:, :, len(SCAN_SEGMENTS + FIN_SEGMENTS) * BR_W:]
    return jnp.concatenate([gates, jnp.zeros((depth, d, LANES - N_GATES), wb.dtype)], axis=-1)


def _token_tile(tt):
    n_blocks = tt // SUB
    return SUB * max(n for n in (1, 2, 3) if n_blocks % n == 0)


def kernel(x, c, ctx, c_ctx, w_mod, b_mod, g_pre, g_post, w_in, w_out, ret_norm_g, sg_w, sg_b,
           sc_conv_w, gdn_conv_w, gdn_a_log, gdn_dt_bias, gdn_norm_g):
    b, t_lat, d = x.shape
    t_ctx = ctx.shape[1]
    depth = w_mod.shape[0]
    assert t_ctx % SUB == 0 and t_lat % SUB == 0 and SUB % RET_CHUNK == 0 and t_lat % GRID_W == 0
    tm = _token_tile(t_ctx + t_lat)
    n_ctx_blocks = t_ctx // SUB
    group = 2 if b % 2 == 0 else 1

    bp = -(-(b + 1) // SUBLANES) * SUBLANES
    cc = jnp.concatenate([c, c_ctx[None, :], jnp.zeros((bp - b - 1, d), F32)], axis=0)
    mod = _modulation(cc, w_mod, b_mod)
    mod = jnp.stack([jnp.broadcast_to(mod[:, :, b:b + 1], (depth, 3, b, d)), mod[:, :, :b]], axis=3)
    mod = mod[:, :, :, :, None, :]

    cos_tab, sin_tab = _rope_tables(t_lat, t_ctx)
    intra, qdec, kdec, cdec = _retention_tables()
    w_in_b = w_in.astype(BF16)
    w_gates = _gate_weights(w_in_b)
    w_out_b = w_out.astype(BF16)
    prm = jnp.zeros((depth, SUBLANES, LANES), F32)
    prm = prm.at[:, 0, :N_GATES // 2].set(gdn_a_log.reshape(depth, -1))
    prm = prm.at[:, 1, :N_GATES // 2].set(gdn_dt_bias.reshape(depth, -1))
    sgw = jnp.transpose(sg_w, (0, 2, 1, 3)).reshape(depth, RET_CHUNK, N_HEADS * RET_CHUNK)
    sgb = jnp.repeat(jnp.swapaxes(sg_b, 1, 2), HEAD_DIM, axis=2)
    gdn_g = jnp.tile(gdn_norm_g, (1, N_HEADS))

    xc = (ctx, x)
    for l in range(depth):
        last = l == depth - 1
        shift, scale, gate = mod[l, 0], mod[l, 1], mod[l, 2]
        qkv, gates = _scan_inputs(xc, shift, scale, g_pre[l][None, :], w_in_b, w_gates, l, cos_tab, sin_tab,
                                  gdn_conv_w[l], prm[l], tm=tm, n_ctx_blocks=n_ctx_blocks)
        o = _scans(qkv, gates, intra, qdec, kdec, cdec, n_ctx_steps=t_ctx // RET_CHUNK, group=group)
        xc = _finish(o, xc, shift, scale, gate, g_pre[l][None, :], g_post[l][None, :], w_in_b, w_out_b, l,
                     sgw[l], sgb[l], sc_conv_w[l], ret_norm_g[l][None, :], gdn_g[l][None, :],
                     tm=SUB if last else tm, n_ctx_blocks=n_ctx_blocks, skip=n_ctx_blocks if last else 0)
    return xc
```

```python
import functools

import jax
import jax.numpy as jnp
import numpy as np
from jax import lax
from jax.experimental import pallas as pl
from jax.experimental.pallas import tpu as pltpu

HEAD_DIM = 64
N_HEADS = 4
BR_W = N_HEADS * HEAD_DIM
GRID_W = 64
RET_CHUNK = 128
GDN_CHUNK = 64
ROPE_BASE = 10000.0
EPS = 1e-6
N_GATES = 16
LANES = 128
SUBLANES = 8
SUB = 256
SCAN_SEGMENTS = (0, 1, 2, 11, 12, 13)
FIN_SEGMENTS = (8, 9, 7, 10, 3, 4, 5, 6, 14)
VMEM_LIMIT = 56 * 1024 * 1024

F32 = jnp.float32
BF16 = jnp.bfloat16
NEG_INF = float("-inf")


def _mm(a, b):
    return jnp.dot(a.astype(BF16), b.astype(BF16), preferred_element_type=F32)


def _mm_nt(a, b):
    return lax.dot_general(a.astype(BF16), b.astype(BF16), (((1,), (1,)), ((), ())),
                           preferred_element_type=F32)


def _mm_tn(a, b):
    return lax.dot_general(a.astype(BF16), b.astype(BF16), (((0,), (0,)), ((), ())),
                           preferred_element_type=F32)


def _iota(shape, dim):
    return lax.broadcasted_iota(jnp.int32, shape, dim)


def _silu(x):
    return x * jax.nn.sigmoid(x)


def _gelu_tanh(x):
    return 0.5 * x * (1.0 + jnp.tanh(0.7978845608028654 * (x + 0.044715 * (x * x * x))))


def _head_mean_matrix():
    r = _iota((BR_W, BR_W), 0) >> 6
    c = _iota((BR_W, BR_W), 1) >> 6
    return jnp.where(r == c, 1.0 / HEAD_DIM, 0.0).astype(F32)


def _blockdiag(x, chunk):
    xb = x.astype(BF16)
    zeros = jnp.zeros((chunk, LANES), BF16)
    low_head = _iota((chunk, LANES), 1) < HEAD_DIM
    blocks = []
    for head in range(N_HEADS):
        half = xb[:, (head // 2) * LANES:(head // 2 + 1) * LANES]
        kept = jnp.where(low_head if head % 2 == 0 else jnp.logical_not(low_head), half, zeros)
        blocks.append(jnp.concatenate([kept, zeros] if head < 2 else [zeros, kept], axis=1))
    return jnp.concatenate(blocks, axis=0)


def _interleave(chains):
    results = [None] * len(chains)
    live = list(enumerate(chains))
    while live:
        still = []
        for idx, chain in live:
            try:
                next(chain)
                still.append((idx, chain))
            except StopIteration as stop:
                results[idx] = stop.value
        live = still
    return results


def _token_tile_specs(tm, d, tt, first_block=0):
    n_sub, hb = tm // SUB, SUB // SUBLANES
    blocks = [pl.BlockSpec((None, SUB, d), lambda bi, i, s=s: (bi, first_block + i * n_sub + s, 0))
              for s in range(n_sub)]
    prev = pl.BlockSpec((None, SUBLANES, d),
                        lambda bi, i: (bi, jnp.maximum((first_block + i * n_sub) * hb - 1, 0), 0))
    nxt = pl.BlockSpec((None, SUBLANES, d),
                       lambda bi, i: (bi, jnp.minimum((first_block + (i + 1) * n_sub) * hb, tt // SUBLANES - 1), 0))
    mod = pl.BlockSpec((None, 2, 1, d), lambda bi, i: (bi, 0, 0, 0))
    return blocks + [prev, nxt], mod


def _split_tile_specs(tm, d, t_ctx, t_lat):
    n_sub, n_ctx, hb = tm // SUB, t_ctx // SUB, SUB // SUBLANES

    def rows8(first, t):
        return pl.BlockSpec((None, SUBLANES, d),
                            lambda bi, i: (bi, jnp.clip(first(i), 0, t // SUBLANES - 1), 0))

    ctx_blocks = [pl.BlockSpec((None, SUB, d), lambda bi, i, s=s: (bi, jnp.minimum(i * n_sub + s, n_ctx - 1), 0))
                  for s in range(n_sub)]
    lat_blocks = [pl.BlockSpec((None, SUB, d), lambda bi, i, s=s: (bi, jnp.maximum(i * n_sub + s - n_ctx, 0), 0))
                  for s in range(n_sub)]
    halos = [rows8(lambda i: i * n_sub * hb - 1, t_ctx), rows8(lambda i: (i * n_sub - n_ctx) * hb - 1, t_lat),
             rows8(lambda i: (i + 1) * n_sub * hb, t_ctx), rows8(lambda i: ((i + 1) * n_sub - n_ctx) * hb, t_lat)]
    return ctx_blocks + lat_blocks + halos


def _stream_sources(stream, tm, first_block=0):
    if isinstance(stream, tuple):
        ctx, lat = stream
        (b, t_ctx, d), t_lat = ctx.shape, lat.shape[1]
        n_sub = tm // SUB
        specs = _split_tile_specs(tm, d, t_ctx, t_lat)
        operands = [ctx] * n_sub + [lat] * n_sub + [ctx, lat, ctx, lat]
        mod = pl.BlockSpec((None, 2, 1, d), lambda bi, i: (bi, 0, 0, 0))
        return specs, operands, mod, b, t_ctx + t_lat, d
    b, tt, d = stream.shape
    specs, mod = _token_tile_specs(tm, d, tt, first_block)
    return specs, [stream] * len(specs), mod, b, tt, d


def _tile_sources(refs, n_sub, n_ctx_blocks, split):
    if not split:
        return ([refs[s][...] for s in range(n_sub)], refs[n_sub][...], refs[n_sub + 1][...],
                refs[n_sub + 2:])
    ctx_refs, lat_refs = refs[:n_sub], refs[n_sub:2 * n_sub]
    ctx_prev, lat_prev, ctx_next, lat_next = refs[2 * n_sub:2 * n_sub + 4]
    blk0 = pl.program_id(1) * n_sub
    xs = [jnp.where(blk0 + s < n_ctx_blocks, ctx_refs[s][...], lat_refs[s][...]) for s in range(n_sub)]
    prev8 = jnp.where(blk0 - 1 < n_ctx_blocks, ctx_prev[...], lat_prev[...])
    next8 = jnp.where(blk0 + n_sub < n_ctx_blocks, ctx_next[...], lat_next[...])
    return xs, prev8, next8, refs[2 * n_sub + 4:]


def _sub_block(s, xs, prev8, next8, shift_ref, scale_ref, g_ref, *, n_blocks, n_ctx_blocks, first_block=0):
    n_sub = len(xs)
    blk = first_block + pl.program_id(1) * n_sub + s
    before = prev8 if s == 0 else xs[s - 1][SUB - SUBLANES:, :]
    after = next8 if s == n_sub - 1 else xs[s + 1][:SUBLANES, :]
    x = jnp.concatenate([xs[s], before, after], axis=0)
    is_ctx = blk < n_ctx_blocks
    scale = jnp.where(is_ctx, scale_ref[0], scale_ref[1])
    shift = jnp.where(is_ctx, shift_ref[0], shift_ref[1])
    gain = g_ref[...] * (1.0 + scale)
    h = (x * lax.rsqrt(jnp.mean(x * x, axis=-1, keepdims=True) + EPS) * gain + shift).astype(BF16)
    has_prev = jnp.logical_and(blk > 0, blk != n_ctx_blocks)
    has_next = jnp.logical_and(blk < n_blocks - 1, blk != n_ctx_blocks - 1)
    return h, has_prev, has_next, is_ctx


def _conv3(ext, has_prev, has_next, w):
    x = ext[:SUB]
    prev_row = jnp.where(has_prev, ext[SUB + SUBLANES - 1:SUB + SUBLANES, :], 0.0)
    next_row = jnp.where(has_next, ext[SUB + SUBLANES:SUB + SUBLANES + 1, :], 0.0)
    row = _iota(x.shape, 0)
    x_prev = jnp.where(row == 0, prev_row, pltpu.roll(x, 1, axis=0))
    x_next = jnp.where(row == SUB - 1, next_row, pltpu.roll(x, SUB - 1, axis=0))
    return x_prev * w[0:1, :] + x * w[1:2, :] + x_next * w[2:3, :]


def _mod_kernel(cc_ref, w_ref, b_ref, o_ref):
    s = _silu(cc_ref[...])
    o_ref[...] = _mm(s, w_ref[...]) + b_ref[...]


def _modulation(cc, w_mod, b_mod):
    depth, d, _ = w_mod.shape
    bp = cc.shape[0]
    return pl.pallas_call(
        _mod_kernel,
        grid=(depth, 3),
        in_specs=[
            pl.BlockSpec((bp, d), lambda l, n: (0, 0)),
            pl.BlockSpec((None, d, d), lambda l, n: (l, 0, n)),
            pl.BlockSpec((None, None, 1, d), lambda l, n: (l, n, 0, 0)),
        ],
        out_specs=pl.BlockSpec((None, None, bp, d), lambda l, n: (l, n, 0, 0)),
        out_shape=jax.ShapeDtypeStruct((depth, 3, bp, d), F32),
        compiler_params=pltpu.CompilerParams(vmem_limit_bytes=VMEM_LIMIT),
        name="modulation",
    )(cc, w_mod, b_mod.reshape(depth, 3, 1, d))


def _scan_inputs_sub_block(s, xs, prev8, next8, shift_ref, scale_ref, g_ref, wq_ref, wk_ref, wv_ref, gq_ref,
                           gk_ref, gv_ref, wgate_ref, cos_ref, sin_ref, cw_ref, prm_ref, qkv_ref, gate_ref,
                           *, n_blocks, n_ctx_blocks):
    rows = slice(s * SUB, (s + 1) * SUB)
    h, has_prev, has_next, _ = _sub_block(s, xs, prev8, next8, shift_ref, scale_ref, g_ref,
                                          n_blocks=n_blocks, n_ctx_blocks=n_ctx_blocks)
    ret = jnp.concatenate([jnp.dot(h[:SUB], w[...], preferred_element_type=F32)
                           for w in (wq_ref, wk_ref, wv_ref)], axis=1)
    gdn_ext = jnp.concatenate([jnp.dot(h, w[...], preferred_element_type=F32)
                               for w in (gq_ref, gk_ref, gv_ref)], axis=1)
    a = jnp.dot(h[:SUB], wgate_ref[...], preferred_element_type=F32)
    yield

    cos = cos_ref[rows, :]
    sin = sin_ref[rows, :]
    first_half = (_iota(cos.shape, 1) & 31) < 16

    def rope(x):
        partner = jnp.where(first_half, pltpu.roll(x, BR_W - 16, axis=1), pltpu.roll(x, 16, axis=1))
        return x * cos + partner * sin

    qkv_ref[rows, 0 * BR_W:1 * BR_W] = rope(ret[:, 0 * BR_W:1 * BR_W]).astype(BF16)
    qkv_ref[rows, 1 * BR_W:2 * BR_W] = (rope(ret[:, 1 * BR_W:2 * BR_W]) * HEAD_DIM ** -0.5).astype(BF16)
    qkv_ref[rows, 2 * BR_W:3 * BR_W] = ret[:, 2 * BR_W:3 * BR_W].astype(BF16)

    g = _silu(_conv3(gdn_ext, has_prev, has_next, cw_ref[...]))
    hsum = _head_mean_matrix() * HEAD_DIM

    def l2n(x):
        return x * lax.rsqrt(_mm(x * x, hsum) + EPS)

    qkv_ref[rows, 3 * BR_W:4 * BR_W] = (l2n(g[:, 0:BR_W]) * HEAD_DIM ** -0.5).astype(BF16)
    qkv_ref[rows, 4 * BR_W:5 * BR_W] = l2n(g[:, BR_W:2 * BR_W]).astype(BF16)
    qkv_ref[rows, 5 * BR_W:6 * BR_W] = g[:, 2 * BR_W:3 * BR_W].astype(BF16)

    neg_rate = -jnp.exp(prm_ref[0:1, :])
    z = a + prm_ref[1:2, :]
    softplus = jnp.maximum(z, 0.0) + jnp.log1p(jnp.exp(-jnp.abs(z)))
    gate_ref[rows, :] = jnp.where(_iota(a.shape, 1) < N_GATES // 2, neg_rate * softplus, jax.nn.sigmoid(a))


def _scan_inputs_kernel(*refs, n_sub, n_blocks, n_ctx_blocks, split):
    xs, prev8, next8, rest = _tile_sources(refs, n_sub, n_ctx_blocks, split)
    _interleave([_scan_inputs_sub_block(s, xs, prev8, next8, *rest, n_blocks=n_blocks, n_ctx_blocks=n_ctx_blocks)
                 for s in range(n_sub)])


def _segment_specs(layer, d, segments):
    return [pl.BlockSpec((None, d, BR_W), lambda bi, i, n=n: (layer, 0, n)) for n in segments]


def _scan_inputs(stream, shift, scale, g_pre, w_in, w_gates, layer, cos_tab, sin_tab, conv_w, prm,
                 *, tm, n_ctx_blocks):
    specs, operands, mod, b, tt, d = _stream_sources(stream, tm)
    const2 = lambda bi, i: (0, 0)
    kern = functools.partial(_scan_inputs_kernel, n_sub=tm // SUB, n_blocks=tt // SUB,
                             n_ctx_blocks=n_ctx_blocks, split=isinstance(stream, tuple))
    return pl.pallas_call(
        kern,
        grid=(b, tt // tm),
        in_specs=[
            *specs, mod, mod,
            pl.BlockSpec((1, d), const2),
            *_segment_specs(layer, d, SCAN_SEGMENTS),
            pl.BlockSpec((None, d, LANES), lambda bi, i: (layer, 0, 0)),
            pl.BlockSpec((tm, BR_W), lambda bi, i: (i, 0)),
            pl.BlockSpec((tm, BR_W), lambda bi, i: (i, 0)),
            pl.BlockSpec((3, 3 * BR_W), const2),
            pl.BlockSpec((SUBLANES, LANES), const2),
        ],
        out_specs=[
            pl.BlockSpec((None, tm, 6 * BR_W), lambda bi, i: (bi, i, 0)),
            pl.BlockSpec((None, tm, LANES), lambda bi, i: (bi, i, 0)),
        ],
        out_shape=[
            jax.ShapeDtypeStruct((b, tt, 6 * BR_W), BF16),
            jax.ShapeDtypeStruct((b, tt, LANES), F32),
        ],
        compiler_params=pltpu.CompilerParams(vmem_limit_bytes=VMEM_LIMIT),
        name="scan_inputs",
    )(*operands, shift, scale, g_pre, *[w_in] * len(SCAN_SEGMENTS), w_gates, cos_tab, sin_tab, conv_w, prm)


def _chunk_cumsum(x, chunk, reverse):
    t = x.shape[0]
    pos = _iota(x.shape, 0) & (chunk - 1)
    s = 1
    while s < chunk:
        if reverse:
            x = x + jnp.where(pos < chunk - s, pltpu.roll(x, t - s, axis=0), 0.0)
        else:
            x = x + jnp.where(pos >= s, pltpu.roll(x, s, axis=0), 0.0)
        s *= 2
    return x


def _expand_exact(x, e):
    hi = x.astype(BF16)
    r1 = x - hi.astype(F32)
    mid = r1.astype(BF16)
    lo = (r1 - mid.astype(F32)).astype(BF16)
    eb = e.astype(BF16)
    return (jnp.dot(hi, eb, preferred_element_type=F32) + jnp.dot(mid, eb, preferred_element_type=F32)
            + jnp.dot(lo, eb, preferred_element_type=F32))


def _retention_chain(q, k, v, s_ref, o_ref, intra, qdec, kdec, cdec, bd_mask):
    scores = _mm_nt(q, _blockdiag(k, RET_CHUNK)) * intra
    yield
    s = s_ref[...]
    o_ref[...] += _mm(scores, _blockdiag(v, RET_CHUNK)) + _mm(q, s) * qdec
    yield
    s_ref[...] = s * cdec + jnp.where(bd_mask, _mm_tn(k * kdec, v), 0.0)


def _unit_triangular_inverse(a, row, col, eye):
    c = GDN_CHUNK
    same16 = (row >> 4) == (col >> 4)
    same32 = (row >> 5) == (col >> 5)
    a16 = jnp.where(same16, a, 0.0)
    p = jnp.where(eye, 1.0, 0.0) - a16
    x = _mm(a16, _blockdiag(a16, c))
    yield
    for _ in range(2):
        px = _mm(jnp.concatenate([p, x], axis=0), _blockdiag(x, c))
        yield
        p = p + px[:c]
        x = px[c:]
    p = p + _mm(p, _blockdiag(x, c))
    yield
    for off in (jnp.where(jnp.logical_and(same32, jnp.logical_not(same16)), a, 0.0),
                jnp.where(same32, 0.0, a)):
        y = _mm(p, _blockdiag(off, c))
        yield
        p = p - _mm(y, _blockdiag(p, c))
        yield
    return p


def _gdn_chunk_operands(q, k, v, gc, beta, reverse):
    c = GDN_CHUNK
    row = _iota((c, BR_W), 0)
    col = _iota((c, BR_W), 1) & (c - 1)
    eye = row == col
    incl = (row <= col) if reverse else (row >= col)
    strict = (row < col) if reverse else (row > col)
    gc_row = jnp.sum(jnp.where(eye, gc, 0.0), axis=0, keepdims=True)
    decay = jnp.exp(jnp.where(incl, gc - gc_row, NEG_INF))
    kk_qk = _mm_nt(jnp.concatenate([k, q], axis=0), _blockdiag(k, c))
    yield
    a = jnp.where(strict, kk_qk[:c] * beta * decay, 0.0)
    inv = yield from _unit_triangular_inverse(a, row, col, eye)
    egc = jnp.exp(gc)
    u = _mm(inv, _blockdiag(v * beta, c))
    w = _mm(inv, _blockdiag(k * beta * egc, c))
    yield
    g_last = gc[0:1, :] if reverse else gc[c - 1:c, :]
    return dict(u=u, w=w, qd=q * egc, intra=kk_qk[c:] * decay, k_tail=k * jnp.exp(g_last - gc),
                cdec=jnp.exp(g_last))


def _gdn_recurrence_chain(chunks, s_ref, bd_mask):
    c = GDN_CHUNK
    s = s_ref[...]
    for ch, o_ref in chunks:
        ws_qs = _mm(jnp.concatenate([ch["w"], ch["qd"]], axis=0), s)
        yield
        v_new = ch["u"] - ws_qs[:c]
        o_ref[...] += ws_qs[c:] + _mm(ch["intra"], _blockdiag(v_new, c))
        s = s * ch["cdec"] + jnp.where(bd_mask, _mm_tn(ch["k_tail"], v_new), 0.0)
        yield
    s_ref[...] = s


def _bwd_block(j, n_steps, n_ctx_steps):
    return jnp.where(j < n_ctx_steps, n_ctx_steps - 1 - j, n_steps - 1 - (j - n_ctx_steps))


def _scan_kernel(qkv_f_ref, qkv_b_ref, gate_f_ref, gate_b_ref, intra_ref, qdec_ref, kdec_ref,
                 cdec_ref, o_ref, s_ret_ref, s_gdn_ref, *, n_steps, n_ctx_steps):
    j = pl.program_id(1)

    @pl.when(j == 0)
    def _():
        s_ret_ref[...] = jnp.zeros_like(s_ret_ref)
        s_gdn_ref[...] = jnp.zeros_like(s_gdn_ref)
        o_ref[...] = jnp.zeros_like(o_ref)

    bd_mask = (_iota((BR_W, BR_W), 0) >> 6) == (_iota((BR_W, BR_W), 1) >> 6)
    erow = _iota((LANES, BR_W), 0)
    ehead = _iota((LANES, BR_W), 1) >> 6
    c = GDN_CHUNK
    n_sub = RET_CHUNK // c
    row0 = (pl.multiple_of(j * RET_CHUNK, RET_CHUNK),
            pl.multiple_of(_bwd_block(j, n_steps, n_ctx_steps) * RET_CHUNK, RET_CHUNK))

    chains, recurrences = [], []
    for bi in range(o_ref.shape[0]):
        for d, (qkv_ref, gate_ref) in enumerate(((qkv_f_ref, gate_f_ref), (qkv_b_ref, gate_b_ref))):
            reverse = d == 1
            qkv = qkv_ref[bi].astype(F32)
            chains.append(_retention_chain(
                qkv[:, 0:BR_W], qkv[:, BR_W:2 * BR_W], qkv[:, 2 * BR_W:3 * BR_W], s_ret_ref.at[bi, d],
                o_ref.at[bi, pl.ds(row0[d], RET_CHUNK), 0:BR_W], intra_ref[d], qdec_ref[d], kdec_ref[d],
                cdec_ref[...], bd_mask))
            gates = gate_ref[bi]
            gc_all = _expand_exact(_chunk_cumsum(gates, c, reverse),
                                   jnp.where(erow == N_HEADS * d + ehead, 1.0, 0.0))
            beta_all = _mm(gates, jnp.where(erow == N_GATES // 2 + N_HEADS * d + ehead, 1.0, 0.0))
            chunk_refs = []
            for sub in (range(n_sub - 1, -1, -1) if reverse else range(n_sub)):
                r0 = sub * c
                chains.append(_gdn_chunk_operands(
                    qkv[r0:r0 + c, 3 * BR_W:4 * BR_W], qkv[r0:r0 + c, 4 * BR_W:5 * BR_W],
                    qkv[r0:r0 + c, 5 * BR_W:6 * BR_W], gc_all[r0:r0 + c], beta_all[r0:r0 + c], reverse))
                chunk_refs.append(o_ref.at[bi, pl.ds(row0[d] + r0, c), BR_W:2 * BR_W])
            recurrences.append((len(chains) - n_sub, chunk_refs, s_gdn_ref.at[bi, d]))

    results = _interleave(chains)
    _interleave([_gdn_recurrence_chain(list(zip(results[first:first + n_sub], refs)), s_ref, bd_mask)
                 for first, refs, s_ref in recurrences])


def _scans(qkv, gates, intra, qdec, kdec, cdec, *, n_ctx_steps, group):
    b, tt, _ = qkv.shape
    c = RET_CHUNK
    ns = tt // c
    fwd = lambda bi, j: (bi, j, 0)
    bwd = lambda bi, j: (bi, _bwd_block(j, ns, n_ctx_steps), 0)
    const3 = lambda bi, j: (0, 0, 0)
    kern = functools.partial(_scan_kernel, n_steps=ns, n_ctx_steps=n_ctx_steps)
    return pl.pallas_call(
        kern,
        grid=(b // group, ns),
        in_specs=[
            pl.BlockSpec((group, c, 6 * BR_W), fwd),
            pl.BlockSpec((group, c, 6 * BR_W), bwd),
            pl.BlockSpec((group, c, LANES), fwd),
            pl.BlockSpec((group, c, LANES), bwd),
            pl.BlockSpec((2, c, N_HEADS * c), const3),
            pl.BlockSpec((2, c, BR_W), const3),
            pl.BlockSpec((2, c, BR_W), const3),
            pl.BlockSpec((1, BR_W), lambda bi, j: (0, 0)),
        ],
        out_specs=pl.BlockSpec((group, tt, 2 * BR_W), lambda bi, j: (bi, 0, 0)),
        out_shape=jax.ShapeDtypeStruct((b, tt, 2 * BR_W), F32),
        scratch_shapes=[
            pltpu.VMEM((group, 2, BR_W, BR_W), F32),
            pltpu.VMEM((group, 2, BR_W, BR_W), F32),
        ],
        compiler_params=pltpu.CompilerParams(
            dimension_semantics=("arbitrary", "arbitrary"), vmem_limit_bytes=VMEM_LIMIT),
        name="bidirectional_scans",
    )(qkv, qkv, gates, gates, intra, qdec, kdec, cdec)


def _finish_sub_block(s, o_ref, xs, prev8, next8, shift_ref, scale_ref, gate_ref, gpre_ref,
                      gpost_ref, wc_ref, wh_ref, *refs, n_blocks, n_ctx_blocks, first_block):
    wfin_refs = refs[:len(FIN_SEGMENTS) - 2]
    wout_ref, sgw_ref, sgb_ref, scw_ref, rng_ref, gng_ref, out_ref, y_ref = refs[len(FIN_SEGMENTS) - 2:]
    rows = slice(s * SUB, (s + 1) * SUB)
    h, has_prev, has_next, is_ctx = _sub_block(s, xs, prev8, next8, shift_ref, scale_ref, gpre_ref,
                                               n_blocks=n_blocks, n_ctx_blocks=n_ctx_blocks, first_block=first_block)
    ch_ext = (jnp.dot(h, wc_ref[...], preferred_element_type=F32)
              * jnp.dot(h, wh_ref[...], preferred_element_type=F32))
    pf = jnp.concatenate([jnp.dot(h[:SUB], w[...], preferred_element_type=F32) for w in wfin_refs], axis=1)
    yield
    sc_b, sc_z, ret_z, sg_u, sg_v, sg_z, gdn_z = (pf[:, n * BR_W:(n + 1) * BR_W] for n in range(7))
    hm = _head_mean_matrix()

    o_ret = o_ref[:, 0:BR_W]
    cen = o_ret - _mm(o_ret, hm)
    var = _mm(cen * cen, hm)
    y_ref[rows, 0:BR_W] = (cen * lax.rsqrt(var + EPS) * rng_ref[...]) * _silu(ret_z)

    u = _gelu_tanh(sg_u)
    v = _gelu_tanh(sg_v)
    vc = v - jnp.mean(v, axis=-1, keepdims=True)
    v = vc * lax.rsqrt(jnp.mean(vc * vc, axis=-1, keepdims=True) + EPS)
    usz = u * _silu(sg_z)
    sgw = sgw_ref[...]
    for r0 in range(0, SUB, RET_CHUNK):
        sg = _mm(sgw, _blockdiag(v[r0:r0 + RET_CHUNK], RET_CHUNK)) + sgb_ref[...]
        y_ref[rows.start + r0:rows.start + r0 + RET_CHUNK, BR_W:2 * BR_W] = usz[r0:r0 + RET_CHUNK] * sg

    conv = _conv3(ch_ext, has_prev, has_next, scw_ref[...])
    y_ref[rows, 2 * BR_W:3 * BR_W] = sc_b * conv * _silu(sc_z)

    o_gdn = o_ref[:, BR_W:2 * BR_W]
    ms = _mm(o_gdn * o_gdn, hm)
    y_ref[rows, 3 * BR_W:4 * BR_W] = (o_gdn * lax.rsqrt(ms + EPS) * gng_ref[...]) * _silu(gdn_z)
    yield

    r = jnp.dot(y_ref[rows, :].astype(BF16), wout_ref[...], preferred_element_type=F32)
    yield
    gate = jnp.where(is_ctx, gate_ref[0], gate_ref[1])
    rn = (r * lax.rsqrt(jnp.mean(r * r, axis=-1, keepdims=True) + EPS)) * gpost_ref[...]
    out_ref[rows, :] = xs[s] + gate * rn


def _finish_kernel(*refs, n_sub, n_blocks, n_ctx_blocks, first_block, split):
    o_refs = refs[:n_sub]
    xs, prev8, next8, rest = _tile_sources(refs[n_sub:], n_sub, n_ctx_blocks, split)
    _interleave([_finish_sub_block(s, o_refs[s], xs, prev8, next8, *rest, n_blocks=n_blocks,
                                   n_ctx_blocks=n_ctx_blocks, first_block=first_block) for s in range(n_sub)])


def _finish(o, stream, shift, scale, gate, g_pre, g_post, w_in, w_out, layer, sgw, sgb, sc_w, ret_g, gdn_g,
            *, tm, n_ctx_blocks, first_block):
    specs, operands, mod, b, tt, d = _stream_sources(stream, tm, first_block)
    const2 = lambda bi, i: (0, 0)
    n_sub = tm // SUB
    kern = functools.partial(_finish_kernel, n_sub=n_sub, n_blocks=tt // SUB, n_ctx_blocks=n_ctx_blocks,
                             first_block=first_block, split=isinstance(stream, tuple))
    return pl.pallas_call(
        kern,
        grid=(b, (tt // SUB - first_block) // n_sub),
        in_specs=[
            *[pl.BlockSpec((None, SUB, 2 * BR_W), lambda bi, i, s=s: (bi, first_block + i * n_sub + s, 0))
              for s in range(n_sub)],
            *specs, mod, mod, mod,
            pl.BlockSpec((1, d), const2),
            pl.BlockSpec((1, d), const2),
            *_segment_specs(layer, d, FIN_SEGMENTS),
            pl.BlockSpec((None, 4 * BR_W, d), lambda bi, i: (layer, 0, 0)),
            pl.BlockSpec((RET_CHUNK, N_HEADS * RET_CHUNK), const2),
            pl.BlockSpec((RET_CHUNK, BR_W), const2),
            pl.BlockSpec((3, BR_W), const2),
            pl.BlockSpec((1, BR_W), const2),
            pl.BlockSpec((1, BR_W), const2),
        ],
        out_specs=pl.BlockSpec((None, tm, d), lambda bi, i: (bi, i, 0)),
        out_shape=jax.ShapeDtypeStruct((b, tt - first_block * SUB, d), F32),
        scratch_shapes=[pltpu.VMEM((tm, 4 * BR_W), F32)],
        compiler_params=pltpu.CompilerParams(vmem_limit_bytes=VMEM_LIMIT),
        name="finish",
    )(*[o] * n_sub, *operands, shift, scale, gate, g_pre, g_post, *[w_in] * len(FIN_SEGMENTS), w_out, sgw, sgb, sc_w,
      ret_g, gdn_g)


def _rope_tables(t_lat, t_ctx):
    nf = HEAD_DIM // 4
    inv = np.float32(ROPE_BASE) ** (-np.arange(nf, dtype=np.float32) / np.float32(nf))
    rows = t_lat // GRID_W
    row = np.repeat(np.arange(rows), GRID_W).astype(np.float32)
    col = np.tile(np.arange(GRID_W), rows).astype(np.float32)
    ang_r = row[:, None] * inv
    ang_c = col[:, None] * inv
    ang = np.concatenate([ang_r, ang_r, ang_c, ang_c], axis=-1)
    cos = np.tile(np.cos(ang), (1, N_HEADS))
    first_half = (np.arange(HEAD_DIM) % (HEAD_DIM // 2)) < nf
    sin = np.tile(np.where(first_half, -np.sin(ang), np.sin(ang)), (1, N_HEADS))
    cos = np.concatenate([np.ones((t_ctx, BR_W), np.float32), cos], axis=0)
    sin = np.concatenate([np.zeros((t_ctx, BR_W), np.float32), sin], axis=0)
    return jnp.asarray(cos, F32), jnp.asarray(sin, F32)


def _retention_tables():
    c = RET_CHUNK
    log_gamma = np.log(1.0 - 2.0 ** (-5.0 - np.arange(N_HEADS, dtype=np.float32))).astype(np.float32)
    pos = np.arange(c, dtype=np.float32)
    lg = log_gamma[:, None]
    diff = pos[:, None] - pos[None, :]
    intra_f = np.exp(np.where(diff >= 0, diff * lg[..., None], -np.inf))
    intra_b = np.swapaxes(intra_f, 1, 2)
    wide = lambda m: np.transpose(m, (1, 0, 2)).reshape(c, N_HEADS * c)
    nat = lambda m: np.repeat(m.T, HEAD_DIM, axis=1)
    q_f = np.exp((pos + 1.0) * lg)
    k_f = np.exp((c - 1.0 - pos) * lg)
    intra = np.stack([wide(intra_f), wide(intra_b)])
    qdec = np.stack([nat(q_f), nat(q_f[:, ::-1])])
    kdec = np.stack([nat(k_f), nat(k_f[:, ::-1])])
    cdec = np.repeat(np.exp(c * log_gamma), HEAD_DIM)[None, :]
    return tuple(jnp.asarray(t, F32) for t in (intra, qdec, kdec, cdec))


def _gate_weights(wb):
    depth, d, _ = wb.shape
    gates = wb[:, :, len(SCAN_SEGMENTS + FIN_SEGMENTS) * BR_W:]
    return jnp.concatenate([gates, jnp.zeros((depth, d, LANES - N_GATES), wb.dtype)], axis=-1)


def _token_tile(tt):
    n_blocks = tt // SUB
    return SUB * max(n for n in (1, 2, 3) if n_blocks % n == 0)


def kernel(x, c, ctx, c_ctx, w_mod, b_mod, g_pre, g_post, w_in, w_out, ret_norm_g, sg_w, sg_b,
           sc_conv_w, gdn_conv_w, gdn_a_log, gdn_dt_bias, gdn_norm_g):
    b, t_lat, d = x.shape
    t_ctx = ctx.shape[1]
    depth = w_mod.shape[0]
    assert t_ctx % SUB == 0 and t_lat % SUB == 0 and SUB % RET_CHUNK == 0 and t_lat % GRID_W == 0
    tm = _token_tile(t_ctx + t_lat)
    n_ctx_blocks = t_ctx // SUB
    group = 2 if b % 2 == 0 else 1

    bp = -(-(b + 1) // SUBLANES) * SUBLANES
    cc = jnp.concatenate([c, c_ctx[None, :], jnp.zeros((bp - b - 1, d), F32)], axis=0)
    mod = _modulation(cc, w_mod, b_mod)
    mod = jnp.stack([jnp.broadcast_to(mod[:, :, b:b + 1], (depth, 3, b, d)), mod[:, :, :b]], axis=3)
    mod = mod[:, :, :, :, None, :]

    cos_tab, sin_tab = _rope_tables(t_lat, t_ctx)
    intra, qdec, kdec, cdec = _retention_tables()
    w_in_b = w_in.astype(BF16)
    w_gates = _gate_weights(w_in_b)
    w_out_b = w_out.astype(BF16)
    prm = jnp.zeros((depth, SUBLANES, LANES), F32)
    prm = prm.at[:, 0, :N_GATES // 2].set(gdn_a_log.reshape(depth, -1))
    prm = prm.at[:, 1, :N_GATES // 2].set(gdn_dt_bias.reshape(depth, -1))
    sgw = jnp.transpose(sg_w, (0, 2, 1, 3)).reshape(depth, RET_CHUNK, N_HEADS * RET_CHUNK)
    sgb = jnp.repeat(jnp.swapaxes(sg_b, 1, 2), HEAD_DIM, axis=2)
    gdn_g = jnp.tile(gdn_norm_g, (1, N_HEADS))

    xc = (ctx, x)
    for l in range(depth):
        last = l == depth - 1
        shift, scale, gate = mod[l, 0], mod[l, 1], mod[l, 2]
        qkv, gates = _scan_inputs(xc, shift, scale, g_pre[l][None, :], w_in_b, w_gates, l, cos_tab, sin_tab,
                                  gdn_conv_w[l], prm[l], tm=tm, n_ctx_blocks=n_ctx_blocks)
        o = _scans(qkv, gates, intra, qdec, kdec, cdec, n_ctx_steps=t_ctx // RET_CHUNK, group=group)
        xc = _finish(o, xc, shift, scale, gate, g_pre[l][None, :], g_post[l][None, :], w_in_b, w_out_b, l,
                     sgw[l], sgb[l], sc_conv_w[l], ret_norm_g[l][None, :], gdn_g[l][None, :],
                     tm=_token_tile(t_lat) if last else tm, n_ctx_blocks=n_ctx_blocks,
                     first_block=n_ctx_blocks if last else 0)
    return xc
```
